```python
import math
import jax, jax.numpy as jnp
from jax import lax
import numpy as np

D_MODEL = 1024
BATCH = 2
SEQ = 16384
DEPTH = 2

N_A_LAYERS = DEPTH // 2
N_B_LAYERS = DEPTH - N_A_LAYERS
D_FF = 2816
A_HEADS = 8
A_KEY_DIM = 128
A_VAL_DIM = D_MODEL // A_HEADS
A_FORGET_DIM = A_HEADS * A_KEY_DIM
A_VAL_WIDTH = A_HEADS * A_VAL_DIM
A_IN_WIDTH = 2 * A_FORGET_DIM + 2 * A_VAL_WIDTH
A_CHUNK = 64
B_WINDOWS = (128, 512, 2048)
B_DILATIONS = (1, 4, 16)
B_GROUPS = 3
B_HEADS = 16
B_HEAD_DIM = D_MODEL // B_HEADS
B_QKV_WIDTH = B_GROUPS * B_HEADS * B_HEAD_DIM
B_BLOCK = 128
NUM_BUCKETS = 32
MAX_DISTANCE = 2048
EPS = 1e-6

kernel_name = 'yoco_hgrn2_dilated_macaron'


def _rms(x, gain):
    xf = x.astype(jnp.float32)
    y = xf * lax.rsqrt(jnp.mean(xf * xf, axis=-1, keepdims=True) + EPS)
    return (y * gain.astype(jnp.float32)).astype(x.dtype)


def _swiglu(h, w_in, w_out):
    gate, up = jnp.split(h @ w_in, 2, axis=-1)
    return (jax.nn.silu(gate) * up) @ w_out


def _lower_bounds(lb_logits):
    p = jax.nn.softmax(lb_logits.astype(jnp.float32), axis=0)
    return jnp.cumsum(p, axis=0)[:-1]


def _hgrn2(h, w_in, lb, out_gain, w_out):
    bsz, seq, _ = h.shape
    n_chunks = seq // A_CHUNK
    f32 = jnp.float32
    q, f, i, g = jnp.split(h @ w_in, [A_FORGET_DIM, 2 * A_FORGET_DIM, 2 * A_FORGET_DIM + A_VAL_WIDTH], axis=-1)
    log_f = jnp.logaddexp(jnp.log(lb), jnp.log1p(-lb) + jax.nn.log_sigmoid(f.astype(f32)))
    k = -jnp.expm1(log_f)
    q = jax.nn.silu(q.astype(f32))
    v = i.astype(f32)

    def to_chunks(t, dh):
        return t.reshape(bsz, n_chunks, A_CHUNK, A_HEADS, dh).transpose(1, 0, 3, 2, 4)

    qc, kc, gc = (to_chunks(t, A_KEY_DIM) for t in (q, k, log_f))
    vc = to_chunks(v, A_VAL_DIM)
    causal = jnp.tril(jnp.ones((A_CHUNK, A_CHUNK), dtype=bool))

    def step(state, inp):
        qb, kb, vb, gb = inp
        cum = jnp.cumsum(gb, axis=2)
        rel = cum[:, :, :, None, :] - cum[:, :, None, :, :]
        decay = jnp.exp(jnp.where(causal[:, :, None], rel, -jnp.inf))
        scores = jnp.einsum('bhtd,bhtsd,bhsd->bhts', qb, decay, kb)
        out = (jnp.einsum('bhts,bhse->bhte', scores, vb)
               + jnp.einsum('bhtd,bhde->bhte', qb * jnp.exp(cum), state))
        last = cum[:, :, -1, :]
        state = (jnp.exp(last)[..., None] * state
                 + jnp.einsum('bhsd,bhse->bhde', kb * jnp.exp(last[:, :, None, :] - cum), vb))
        return state, out

    state0 = jnp.zeros((bsz, A_HEADS, A_KEY_DIM, A_VAL_DIM), f32)
    _, o = lax.scan(step, state0, (qc, kc, vc, gc))
    o = o.transpose(1, 0, 3, 2, 4).reshape(bsz, seq, A_HEADS, A_VAL_DIM)
    o = _rms(o, out_gain) * jax.nn.silu(g.astype(f32)).reshape(bsz, seq, A_HEADS, A_VAL_DIM)
    return o.reshape(bsz, seq, A_VAL_WIDTH).astype(h.dtype) @ w_out


def _t5_bucket(dist):
    dist = np.asarray(dist, np.int32)
    max_exact = NUM_BUCKETS // 2
    large = max_exact + (np.log(np.maximum(dist, 1) / max_exact)
                         / math.log(MAX_DISTANCE / max_exact) * (NUM_BUCKETS - max_exact)).astype(np.int32)
    large = np.minimum(large, NUM_BUCKETS - 1)
    return np.where(dist < max_exact, dist, large).astype(np.int32)


def _band_static(window, dilation):
    nq = B_BLOCK // dilation
    wd = window // dilation
    nj = wd + nq
    m = np.arange(nj)[None, :] - np.arange(nq)[:, None]
    band = (m >= 0) & (m <= wd)
    bucket = _t5_bucket((wd - np.clip(m, 0, wd)) * dilation)
    return band, bucket


def _shared_kv(x, kv_norm, w_kv, k_gain):
    bsz, seq, _ = x.shape
    kv = (_rms(x, kv_norm) @ w_kv).reshape(bsz, seq, 2, B_GROUPS, B_HEADS, B_HEAD_DIM)
    k = _rms(kv[:, :, 0], k_gain[:, None, :])
    v = kv[:, :, 1]
    k_pads = [jnp.pad(k[:, :, g], ((0, 0), (w, 0), (0, 0), (0, 0))) for g, w in enumerate(B_WINDOWS)]
    v_pads = [jnp.pad(v[:, :, g], ((0, 0), (w, 0), (0, 0), (0, 0))) for g, w in enumerate(B_WINDOWS)]
    return k_pads, v_pads


def _dilated_attention(h, w_q, q_gain, w_o, k_pads, v_pads, rel_bias):
    bsz, seq, _ = h.shape
    n_blocks = seq // B_BLOCK
    f32 = jnp.float32
    q = (h @ w_q).reshape(bsz, seq, B_GROUPS, B_HEADS, B_HEAD_DIM)
    q = _rms(q, q_gain[:, None, :]) * (B_HEAD_DIM ** -0.5)
    q_groups = [q[:, :, g] for g in range(B_GROUPS)]
    statics = [_band_static(w, d) for w, d in zip(B_WINDOWS, B_DILATIONS)]
    biases = [rel_bias[bucket][..., g * B_HEADS:(g + 1) * B_HEADS].transpose(2, 0, 1).astype(f32)
              for g, (_, bucket) in enumerate(statics)]

    def block(b):
        start = b * B_BLOCK
        outs, lses = [], []
        for g, (window, dil) in enumerate(zip(B_WINDOWS, B_DILATIONS)):
            band = statics[g][0]
            nq = B_BLOCK // dil
            nj = window // dil + nq
            qb = lax.dynamic_slice_in_dim(q_groups[g], start, B_BLOCK, axis=1).reshape(bsz, nq, dil, B_HEADS, B_HEAD_DIM)
            kb = lax.dynamic_slice_in_dim(k_pads[g], start, window + B_BLOCK, axis=1).reshape(bsz, nj, dil, B_HEADS, B_HEAD_DIM)
            vb = lax.dynamic_slice_in_dim(v_pads[g], start, window + B_BLOCK, axis=1).reshape(bsz, nj, dil, B_HEADS, B_HEAD_DIM)
            s = jnp.einsum('birhd,bjrhd->bhrij', qb, kb).astype(f32) + biases[g][None, :, None]
            pos = start - window + jnp.arange(nj)[:, None] * dil + jnp.arange(dil)[None, :]
            valid = band[None, :, :] & (pos.T >= 0)[:, None, :]
            s = jnp.where(valid[None, None], s, -jnp.inf)
            lse = jax.nn.logsumexp(s, axis=-1)
            p = jnp.exp(s - lse[..., None]).astype(vb.dtype)
            o = jnp.einsum('bhrij,bjrhd->birhd', p, vb).reshape(bsz, B_BLOCK, B_HEADS, B_HEAD_DIM)
            outs.append(o.astype(f32))
            lses.append(lse.transpose(0, 3, 2, 1).reshape(bsz, B_BLOCK, B_HEADS))
        wts = jax.nn.softmax(jnp.stack(lses), axis=0)
        o = jnp.einsum('gbqh,gbqhd->bqhd', wts, jnp.stack(outs))
        return o.reshape(bsz, B_BLOCK, B_HEADS * B_HEAD_DIM).astype(h.dtype)

    o = lax.map(block, jnp.arange(n_blocks))
    o = o.transpose(1, 0, 2, 3).reshape(bsz, seq, B_HEADS * B_HEAD_DIM)
    return o @ w_o


def setup_inputs(seed: int = 0) -> dict:
    key = jax.random.key(seed)
    ks = jax.random.split(key, 16)

    def dense(k, shape, fan_in):
        return jax.random.normal(k, shape, jnp.float32) * fan_in ** -0.5

    def gain(k, shape):
        return 1.0 + 0.1 * jax.random.normal(k, shape, jnp.float32)

    return {
        'x': jax.random.normal(ks[0], (BATCH, SEQ, D_MODEL), jnp.float32),
        'norm_gain': gain(ks[1], (DEPTH, 3, D_MODEL)),
        'ffn_w_in': dense(ks[2], (DEPTH, 2, D_MODEL, 2 * D_FF), D_MODEL),
        'ffn_w_out': dense(ks[3], (DEPTH, 2, D_FF, D_MODEL), D_FF),
        'a_w_in': dense(ks[4], (N_A_LAYERS, D_MODEL, A_IN_WIDTH), D_MODEL),
        'a_lb_logits': 0.5 * jax.random.normal(ks[5], (N_A_LAYERS + 1, A_FORGET_DIM), jnp.float32),
        'a_out_gain': gain(ks[6], (N_A_LAYERS, A_VAL_DIM)),
        'a_w_out': dense(ks[7], (N_A_LAYERS, A_VAL_WIDTH, D_MODEL), A_VAL_WIDTH),
        'kv_norm': gain(ks[8], (D_MODEL,)),
        'w_kv': dense(ks[9], (D_MODEL, 2 * B_QKV_WIDTH), D_MODEL),
        'k_gain': gain(ks[10], (B_GROUPS, B_HEAD_DIM)),
        'b_w_q': dense(ks[11], (N_B_LAYERS, D_MODEL, B_QKV_WIDTH), D_MODEL),
        'b_q_gain': gain(ks[12], (N_B_LAYERS, B_GROUPS, B_HEAD_DIM)),
        'b_w_o': dense(ks[13], (N_B_LAYERS, B_HEADS * B_HEAD_DIM, D_MODEL), B_HEADS * B_HEAD_DIM),
        'rel_bias': 0.5 * jax.random.normal(ks[14], (NUM_BUCKETS, B_GROUPS * B_HEADS), jnp.float32),
    }


def reference(x, norm_gain, ffn_w_in, ffn_w_out, a_w_in, a_lb_logits, a_out_gain, a_w_out,
              kv_norm, w_kv, k_gain, b_w_q, b_q_gain, b_w_o, rel_bias):
    lower_bounds = _lower_bounds(a_lb_logits)
    k_pads, v_pads = None, None
    for layer in range(DEPTH):
        if layer == N_A_LAYERS:
            k_pads, v_pads = _shared_kv(x, kv_norm, w_kv, k_gain)
        x = x + 0.5 * _swiglu(_rms(x, norm_gain[layer, 0]), ffn_w_in[layer, 0], ffn_w_out[layer, 0])
        h = _rms(x, norm_gain[layer, 1])
        if layer < N_A_LAYERS:
            x = x + _hgrn2(h, a_w_in[layer], lower_bounds[layer], a_out_gain[layer], a_w_out[layer])
        else:
            j = layer - N_A_LAYERS
            x = x + _dilated_attention(h, b_w_q[j], b_q_gain[j], b_w_o[j], k_pads, v_pads, rel_bias)
        x = x + 0.5 * _swiglu(_rms(x, norm_gain[layer, 2]), ffn_w_in[layer, 1], ffn_w_out[layer, 1])
    return x
```

```python
import functools
import math

import jax
import jax.numpy as jnp
import numpy as np
from jax import lax
from jax.experimental import pallas as pl
from jax.experimental.pallas import tpu as pltpu

F32 = jnp.float32
BF16 = jnp.bfloat16

EPS = 1e-6
LANES = 128
MXU_COLS = 256
VMEM_LIMIT = 56 * 1024 * 1024

A_HEADS = 8
A_DIM = 128
A_CHUNK = 128
A_DIAG = 16
A_LEVELS = (32, 64, 128)

B_WINDOWS = (128, 512, 2048)
B_DILATIONS = (1, 4, 16)
B_GROUPS = 3
B_HEADS = 16
B_HEAD_DIM = 64
B_BLOCK = 128
NUM_BUCKETS = 32
MAX_DISTANCE = 2048
NEG = -1e30


def _const_spec(shape):
    nd = len(shape)
    return pl.BlockSpec(shape, lambda *_: (0,) * nd, pipeline_mode=pl.Buffered(1))


def _sigmoid(x):
    return 1.0 / (1.0 + jnp.exp(-x))


def _rms_rows(x, gain):
    ms = jnp.mean(x * x, axis=-1, keepdims=True)
    return x * lax.rsqrt(ms + EPS) * gain


def _half_lane_mask(shape):
    return lax.broadcasted_iota(jnp.int32, shape, len(shape) - 1) < (LANES // 2)


def _head_rms_pair(t, gain_row):
    lo = _half_lane_mask(t.shape)
    sq = t * t
    s_lo = jnp.sum(jnp.where(lo, sq, 0.0), axis=-1, keepdims=True)
    s_hi = jnp.sum(jnp.where(lo, 0.0, sq), axis=-1, keepdims=True)
    ms = jnp.where(lo, s_lo, s_hi) * (1.0 / B_HEAD_DIM)
    return t * lax.rsqrt(ms + EPS) * gain_row


def _ffn_kernel(x_ref, gain_ref, wg_ref, wu_ref, wo_ref, o_ref, act_ref, *, ff_chunk):
    x = x_ref[...]
    h = _rms_rows(x, gain_ref[...]).astype(BF16)
    d_ff = wg_ref.shape[1]
    for c0 in range(0, d_ff, ff_chunk):
        sl = slice(c0, c0 + ff_chunk)
        g = jnp.dot(h, wg_ref[:, sl], preferred_element_type=F32)
        u = jnp.dot(h, wu_ref[:, sl], preferred_element_type=F32)
        act_ref[:, sl] = (g * _sigmoid(g) * u).astype(BF16)
    y = jnp.dot(act_ref[...], wo_ref[...], preferred_element_type=F32)
    o_ref[...] = x + 0.5 * y


def _ffn(x2, gain, w_in, w_out, *, tm=512, ff_chunk=256):
    t, d = x2.shape
    d_ff = w_out.shape[0]
    wg = w_in[:, :d_ff].astype(BF16)
    wu = w_in[:, d_ff:].astype(BF16)
    wo = w_out.astype(BF16)
    return pl.pallas_call(
        functools.partial(_ffn_kernel, ff_chunk=ff_chunk),
        grid=(t // tm,),
        in_specs=[
            pl.BlockSpec((tm, d), lambda i: (i, 0)),
            _const_spec((1, d)),
            _const_spec((d, d_ff)),
            _const_spec((d, d_ff)),
            _const_spec((d_ff, d)),
        ],
        out_specs=pl.BlockSpec((tm, d), lambda i: (i, 0)),
        out_shape=jax.ShapeDtypeStruct((t, d), F32),
        scratch_shapes=[pltpu.VMEM((tm, d_ff), BF16)],
        compiler_params=pltpu.CompilerParams(
            dimension_semantics=("arbitrary",), vmem_limit_bytes=VMEM_LIMIT),
        name="swiglu_half_step",
    )(x2, gain.reshape(1, d), wg, wu, wo)


def _block_row_bcast(t, block, row):
    n = t.shape[0]
    t3 = t.reshape(n // block, block, t.shape[1])
    return jnp.broadcast_to(t3[:, row:row + 1, :], t3.shape).reshape(t.shape)


def _hgrn2_chunk(q, k, v, logf, st_ref, hd, tril):
    c = A_CHUNK
    row = lax.broadcasted_iota(jnp.int32, (c, LANES), 0)
    lane = lax.broadcasted_iota(jnp.int32, (c, LANES), 1)

    lf_hi = logf.astype(BF16)
    lf_lo = (logf - lf_hi.astype(F32)).astype(BF16)
    b = (jnp.dot(tril, lf_hi, preferred_element_type=F32)
         + jnp.dot(tril, lf_lo, preferred_element_type=F32))

    a = jnp.zeros((c, LANES), F32)
    rin = jnp.bitwise_and(row, A_DIAG - 1)
    col0 = row - rin
    for s in range(A_DIAG):
        ks = _block_row_bcast(k, A_DIAG, s)
        bs = _block_row_bcast(b, A_DIAG, s)
        valid = rin >= s
        e = jnp.exp(jnp.where(valid, b - bs, NEG))
        col = jnp.sum(q * ks * e, axis=-1, keepdims=True)
        a = jnp.where(lane == col0 + s, col, a)

    for level in A_LEVELS:
        half = level // 2
        b_ref = _block_row_bcast(b, level, half - 1)
        e = jnp.exp(-jnp.abs(b - b_ref))
        upper = jnp.bitwise_and(row, level - 1) >= half
        ql = jnp.where(upper, q * e, 0.0).astype(BF16)
        kl = jnp.where(upper, 0.0, k * e).astype(BF16)
        al = lax.dot_general(ql, kl, (((1,), (1,)), ((), ())), preferred_element_type=F32)
        if level < c:
            same_block = jnp.bitwise_and(row, -level) == jnp.bitwise_and(lane, -level)
            al = jnp.where(same_block, al, 0.0)
        a = a + al

    st = st_ref[hd]
    b_last = b[c - 1:c, :]
    q_in = (q * jnp.exp(b)).astype(BF16)
    o = (jnp.dot(a.astype(BF16), v.astype(BF16), preferred_element_type=F32)
         + lax.dot_general(q_in, st.astype(BF16), (((1,), (1,)), ((), ())),
                           preferred_element_type=F32))
    k_out = (k * jnp.exp(b_last - b)).astype(BF16)
    st_ref[hd] = st * jnp.exp(b_last) + jnp.dot(v.T.astype(BF16), k_out,
                                                 preferred_element_type=F32)
    return o


def _hgrn2_kernel(x_ref, gain_ref, w_ref, lbl_ref, og_ref, wo_ref, tril_ref, o_ref,
                  h_ref, st_ref, oh_ref, *, lb_slots):
    tt = x_ref.shape[1]

    @pl.when(pl.program_id(1) == 0)
    def _():
        st_ref[...] = jnp.zeros_like(st_ref)

    x = x_ref[0]
    h_ref[...] = _rms_rows(x, gain_ref[...]).astype(BF16)
    tril = tril_ref[...]

    def head_body(hd, carry):
        p = jnp.dot(h_ref[...], w_ref[hd], preferred_element_type=F32)
        qp, fp, v, gp = (p[:, i * A_DIM:(i + 1) * A_DIM] for i in range(4))
        lg = lbl_ref[hd]
        ex = jnp.exp(lg - jnp.max(lg, axis=0, keepdims=True))
        lb = (jnp.sum(ex[:lb_slots], axis=0, keepdims=True)
              / jnp.sum(ex, axis=0, keepdims=True))
        z = jnp.exp(-jnp.abs(fp))
        r = 1.0 / (1.0 + z)
        pos = fp >= 0
        sig = jnp.where(pos, r, z * r)
        one_minus_sig = jnp.where(pos, z * r, r)
        logf = jnp.log(lb + (1.0 - lb) * sig)
        k = (1.0 - lb) * one_minus_sig
        q = qp * _sigmoid(qp)
        gate = gp * _sigmoid(gp)
        for c0 in range(0, tt, A_CHUNK):
            sl = slice(c0, c0 + A_CHUNK)
            o = _hgrn2_chunk(q[sl], k[sl], v[sl], logf[sl], st_ref, hd, tril)
            o = _rms_rows(o, og_ref[...]) * gate[sl]
            oh_ref[hd, sl, :] = o.astype(BF16)
        return carry

    lax.fori_loop(0, A_HEADS, head_body, 0)
    o_cat = jnp.concatenate([oh_ref[i] for i in range(A_HEADS)], axis=-1)
    o_ref[0] = x + jnp.dot(o_cat, wo_ref[...], preferred_element_type=F32)


def _hgrn2_layer(x, gain, w_in, lb_logits, layer, out_gain, w_out, *, tt=512):
    bsz, seq, d = x.shape
    n_slots = lb_logits.shape[0]
    w_heads = (w_in.reshape(d, 4, A_HEADS, A_DIM).transpose(2, 0, 1, 3)
               .reshape(A_HEADS, d, 4 * A_DIM).astype(BF16))
    lbl = lb_logits.reshape(n_slots, A_HEADS, A_DIM).transpose(1, 0, 2)
    tril = jnp.asarray(np.tril(np.ones((A_CHUNK, A_CHUNK), np.float32)), BF16)
    return pl.pallas_call(
        functools.partial(_hgrn2_kernel, lb_slots=layer + 1),
        grid=(bsz, seq // tt),
        in_specs=[
            pl.BlockSpec((1, tt, d), lambda b, i: (b, i, 0)),
            _const_spec((1, d)),
            _const_spec((A_HEADS, d, 4 * A_DIM)),
            _const_spec((A_HEADS, n_slots, A_DIM)),
            _const_spec((1, A_DIM)),
            _const_spec((d, d)),
            _const_spec((A_CHUNK, A_CHUNK)),
        ],
        out_specs=pl.BlockSpec((1, tt, d), lambda b, i: (b, i, 0)),
        out_shape=jax.ShapeDtypeStruct((bsz, seq, d), F32),
        scratch_shapes=[
            pltpu.VMEM((tt, d), BF16),
            pltpu.VMEM((A_HEADS, A_DIM, A_DIM), F32),
            pltpu.VMEM((A_HEADS, tt, A_DIM), BF16),
        ],
        compiler_params=pltpu.CompilerParams(
            dimension_semantics=("arbitrary", "arbitrary"), vmem_limit_bytes=VMEM_LIMIT),
        name="hgrn2_layer",
    )(x, gain.reshape(1, d), w_heads, lbl, out_gain.reshape(1, A_DIM), w_out.astype(BF16), tril)


def _proj_kernel(x_ref, gain_ref, w_ref, hg_ref, *out_refs, n_normed, scale):
    h = _rms_rows(x_ref[...], gain_ref[...]).astype(BF16)
    width = out_refs[0].shape[1]
    for j, o_ref in enumerate(out_refs):
        for c0 in range(0, width, MXU_COLS):
            col = j * width + c0
            t2 = jnp.dot(h, w_ref[:, col:col + MXU_COLS], preferred_element_type=F32)
            for c1 in range(c0, c0 + MXU_COLS, LANES):
                t = t2[:, c1 - c0:c1 - c0 + LANES]
                if j < n_normed:
                    t = _head_rms_pair(t, hg_ref[j:j + 1, c1:c1 + LANES]) * scale
                o_ref[:, c1:c1 + LANES] = t.astype(BF16)


def _project(x2, gain, w, head_gain, *, n_out, n_normed, scale, tm=512):
    t, d = x2.shape
    width = w.shape[1] // n_out
    hg = jnp.tile(head_gain, (1, B_HEADS))
    outs = pl.pallas_call(
        functools.partial(_proj_kernel, n_normed=n_normed, scale=scale),
        grid=(t // tm,),
        in_specs=[
            pl.BlockSpec((tm, d), lambda i: (i, 0)),
            _const_spec((1, d)),
            _const_spec((d, n_out * width)),
            _const_spec((n_normed, width)),
        ],
        out_specs=[pl.BlockSpec((tm, width), lambda i: (i, 0)) for _ in range(n_out)],
        out_shape=[jax.ShapeDtypeStruct((t, width), BF16) for _ in range(n_out)],
        compiler_params=pltpu.CompilerParams(
            dimension_semantics=("arbitrary",), vmem_limit_bytes=VMEM_LIMIT),
        name="norm_projection",
    )(x2, gain.reshape(1, d), w.astype(BF16), hg)
    return outs


def _t5_bucket(dist):
    dist = np.asarray(dist, np.int32)
    max_exact = NUM_BUCKETS // 2
    large = max_exact + (np.log(np.maximum(dist, 1) / max_exact)
                         / math.log(MAX_DISTANCE / max_exact) * (NUM_BUCKETS - max_exact)).astype(np.int32)
    large = np.minimum(large, NUM_BUCKETS - 1)
    return np.where(dist < max_exact, dist, large).astype(np.int32)


def _band_bias(rel_bias, group):
    dil = B_DILATIONS[group]
    wd = B_WINDOWS[group] // dil
    i = np.arange(B_BLOCK)[:, None]
    j = np.arange(2 * B_BLOCK)[None, :]
    dist = i + wd - j
    band = (dist >= 0) & (dist <= wd)
    bucket = _t5_bucket(np.clip(dist, 0, wd) * dil)
    bias = rel_bias[:, group * B_HEADS:(group + 1) * B_HEADS][bucket]
    bias = jnp.where(band[:, :, None], bias.astype(F32), NEG)
    bias = bias.transpose(2, 0, 1)
    return bias.reshape(B_HEADS // 2, 2 * B_BLOCK, 2 * B_BLOCK)


def _attn_kernel(q_ref, kp_ref, kc_ref, vp_ref, vc_ref, bias_ref, o_ref, lse_ref):
    nq = B_BLOCK
    first = pl.program_id(2) == 0
    lo = _half_lane_mask((nq, LANES))
    lane = lax.broadcasted_iota(jnp.int32, (nq, LANES), 1)
    key_is_prev = lax.broadcasted_iota(jnp.int32, (2 * nq, 2 * nq), 1) < nq
    prev_pad = jnp.where(jnp.logical_and(first, key_is_prev), NEG, 0.0)
    ones = jnp.ones((2 * nq, LANES), BF16)
    lse_tile = jnp.zeros((nq, LANES), F32)
    for pair in range(B_HEADS // 2):
        sl = slice(pair * LANES, (pair + 1) * LANES)
        qp = q_ref[0, :, sl]
        zero = jnp.zeros_like(qp)
        qq = jnp.concatenate([jnp.where(lo, qp, zero), jnp.where(lo, zero, qp)], axis=0)
        kk = jnp.concatenate([kp_ref[0, :, sl], kc_ref[0, :, sl]], axis=0)
        vv = jnp.concatenate([vp_ref[0, :, sl], vc_ref[0, :, sl]], axis=0)
        s = lax.dot_general(qq, kk, (((1,), (1,)), ((), ())), preferred_element_type=F32)
        s = s + bias_ref[pair] + prev_pad
        m = jnp.max(s, axis=-1, keepdims=True)
        p = jnp.exp(s - m).astype(BF16)
        ov = jnp.dot(p, jnp.concatenate([vv, ones], axis=1), preferred_element_type=F32)
        l = ov[:, LANES:]
        on = ov[:, :LANES] / l
        o_ref[0, :, sl] = jnp.where(lo, on[:nq], on[nq:]).astype(BF16)
        lse = m + jnp.log(l)
        lse_tile = jnp.where(lane == 2 * pair, lse[:nq], lse_tile)
        lse_tile = jnp.where(lane == 2 * pair + 1, lse[nq:], lse_tile)
    lse_ref[0] = lse_tile


def _attn_group(q, k, v, rel_bias, group):
    bsz, seq, width = q.shape
    dil = B_DILATIONS[group]
    n_sub = seq // dil
    n_tiles = n_sub // B_BLOCK
    view = lambda a: a.reshape(bsz, n_sub, dil * a.shape[-1])
    cur = lambda w: pl.BlockSpec((1, B_BLOCK, w), lambda b, r, i: (b, i, r))
    prev = lambda w: pl.BlockSpec((1, B_BLOCK, w), lambda b, r, i: (b, jnp.maximum(i - 1, 0), r))
    o, lse = pl.pallas_call(
        _attn_kernel,
        grid=(bsz, dil, n_tiles),
        in_specs=[cur(width), prev(width), cur(width), prev(width), cur(width),
                  _const_spec((B_HEADS // 2, 2 * B_BLOCK, 2 * B_BLOCK))],
        out_specs=[cur(width), cur(LANES)],
        out_shape=[jax.ShapeDtypeStruct((bsz, n_sub, dil * width), BF16),
                   jax.ShapeDtypeStruct((bsz, n_sub, dil * LANES), F32)],
        compiler_params=pltpu.CompilerParams(
            dimension_semantics=("arbitrary", "arbitrary", "arbitrary"),
            vmem_limit_bytes=VMEM_LIMIT),
        name=f"dilated_attention_g{group}",
    )(view(q), view(k), view(k), view(v), view(v), _band_bias(rel_bias, group))
    return o.reshape(bsz, seq, width), lse.reshape(bsz, seq, LANES)


def _merge_kernel(x_ref, o0_ref, o1_ref, o2_ref, l0_ref, l1_ref, l2_ref, ex_ref, wo_ref, out_ref):
    lses = [l0_ref[...], l1_ref[...], l2_ref[...]]
    m = jnp.maximum(jnp.maximum(lses[0], lses[1]), lses[2])
    es = [jnp.exp(l - m) for l in lses]
    inv = 1.0 / (es[0] + es[1] + es[2])
    acc = None
    for e, o_ref in zip(es, (o0_ref, o1_ref, o2_ref)):
        w = jnp.dot((e * inv).astype(BF16), ex_ref[...], preferred_element_type=F32)
        term = w * o_ref[...].astype(F32)
        acc = term if acc is None else acc + term
    out_ref[...] = x_ref[...] + jnp.dot(acc.astype(BF16), wo_ref[...], preferred_element_type=F32)


def _merge_out(x2, os, lses, w_o, *, tm=512):
    t, d = x2.shape
    width = os[0].shape[-1]
    expand = np.zeros((LANES, width), np.float32)
    for h in range(B_HEADS):
        expand[h, h * B_HEAD_DIM:(h + 1) * B_HEAD_DIM] = 1.0
    row = lambda w: pl.BlockSpec((tm, w), lambda i: (i, 0))
    return pl.pallas_call(
        _merge_kernel,
        grid=(t // tm,),
        in_specs=[row(d), row(width), row(width), row(width), row(LANES), row(LANES), row(LANES),
                  _const_spec((LANES, width)), _const_spec((width, d))],
        out_specs=row(d),
        out_shape=jax.ShapeDtypeStruct((t, d), F32),
        compiler_params=pltpu.CompilerParams(
            dimension_semantics=("arbitrary",), vmem_limit_bytes=VMEM_LIMIT),
        name="merge_out_projection",
    )(x2, *[o.reshape(t, width) for o in os], *[l.reshape(t, LANES) for l in lses],
      jnp.asarray(expand, BF16), w_o.astype(BF16))


def kernel(x, norm_gain, ffn_w_in, ffn_w_out, a_w_in, a_lb_logits, a_out_gain, a_w_out,
           kv_norm, w_kv, k_gain, b_w_q, b_q_gain, b_w_o, rel_bias):
    bsz, seq, d = x.shape
    depth = norm_gain.shape[0]
    n_a = a_w_in.shape[0]
    t = bsz * seq
    ks = vs = None
    for layer in range(depth):
        if layer == n_a:
            outs = _project(x.reshape(t, d), kv_norm, w_kv, k_gain,
                            n_out=2 * B_GROUPS, n_normed=B_GROUPS, scale=1.0)
            ks = [o.reshape(bsz, seq, -1) for o in outs[:B_GROUPS]]
            vs = [o.reshape(bsz, seq, -1) for o in outs[B_GROUPS:]]
        x = _ffn(x.reshape(t, d), norm_gain[layer, 0], ffn_w_in[layer, 0],
                 ffn_w_out[layer, 0]).reshape(bsz, seq, d)
        if layer < n_a:
            x = _hgrn2_layer(x, norm_gain[layer, 1], a_w_in[layer], a_lb_logits, layer,
                             a_out_gain[layer], a_w_out[layer])
        else:
            j = layer - n_a
            qs = _project(x.reshape(t, d), norm_gain[layer, 1], b_w_q[j], b_q_gain[j],
                          n_out=B_GROUPS, n_normed=B_GROUPS, scale=B_HEAD_DIM ** -0.5)
            os, lses = [], []
            for g in range(B_GROUPS):
                o, lse = _attn_group(qs[g].reshape(bsz, seq, -1), ks[g], vs[g], rel_bias, g)
                os.append(o)
                lses.append(lse)
            x = _merge_out(x.reshape(t, d), os, lses, b_w_o[j]).reshape(bsz, seq, d)
        x = _ffn(x.reshape(t, d), norm_gain[layer, 2], ffn_w_in[layer, 1],
                 ffn_w_out[layer, 1]).reshape(bsz, seq, d)
    return x
```

```python
import functools
import math

import jax
import jax.numpy as jnp
import numpy as np
from jax import lax
from jax.experimental import pallas as pl
from jax.experimental.pallas import tpu as pltpu

F32 = jnp.float32
BF16 = jnp.bfloat16

EPS = 1e-6
LOG2E = math.log2(math.e)
F32_TINY = float(np.finfo(np.float32).tiny)
LANES = 128
SUBLANES = 8
MXU_COLS = 256
VMEM_LIMIT = 56 * 1024 * 1024

A_HEADS = 8
A_DIM = 128
A_CHUNK = 128
A_DIAG = SUBLANES
A_LEVELS = (16, 32, 64, 128)

B_WINDOWS = (128, 512, 2048)
B_DILATIONS = (1, 4, 16)
B_GROUPS = 3
B_HEADS = 16
B_HEAD_DIM = 64
B_BLOCK = 128
NUM_BUCKETS = 32
MAX_DISTANCE = 2048
NEG = -1e30

NT_DIMS = (((1,), (1,)), ((), ()))


def _const_spec(shape):
    nd = len(shape)
    return pl.BlockSpec(shape, lambda *_: (0,) * nd, pipeline_mode=pl.Buffered(1))


def _sigmoid(x):
    return 1.0 / (1.0 + jnp.exp(-x))


def _rms_rows(x, gain):
    ms = jnp.mean(x * x, axis=-1, keepdims=True)
    return x * lax.rsqrt(ms + EPS) * gain


def _half_lane_mask(shape):
    return lax.broadcasted_iota(jnp.int32, shape, len(shape) - 1) < (LANES // 2)


def _head_rms_pair(t, gain_row):
    lo = _half_lane_mask(t.shape)
    sq = t * t
    s_lo = jnp.sum(jnp.where(lo, sq, 0.0), axis=-1, keepdims=True)
    s_hi = jnp.sum(jnp.where(lo, 0.0, sq), axis=-1, keepdims=True)
    ms = jnp.where(lo, s_lo, s_hi) * (1.0 / B_HEAD_DIM)
    return t * lax.rsqrt(ms + EPS) * gain_row


def _ffn_kernel(x_ref, gain_ref, wg_ref, wu_ref, wo_ref, o_ref, act_ref, *, ff_chunk):
    x = x_ref[...]
    h = _rms_rows(x, gain_ref[...]).astype(BF16)
    d_ff = wg_ref.shape[1]
    for c0 in range(0, d_ff, ff_chunk):
        sl = slice(c0, c0 + ff_chunk)
        g = jnp.dot(h, wg_ref[:, sl], preferred_element_type=F32)
        u = jnp.dot(h, wu_ref[:, sl], preferred_element_type=F32)
        act_ref[:, sl] = (g * _sigmoid(g) * u).astype(BF16)
    y = jnp.dot(act_ref[...], wo_ref[...], preferred_element_type=F32)
    o_ref[...] = x + 0.5 * y


def _ffn(x2, gain, w_in, w_out, *, tm=512, ff_chunk=256):
    t, d = x2.shape
    d_ff = w_out.shape[0]
    wg = w_in[:, :d_ff].astype(BF16)
    wu = w_in[:, d_ff:].astype(BF16)
    wo = w_out.astype(BF16)
    return pl.pallas_call(
        functools.partial(_ffn_kernel, ff_chunk=ff_chunk),
        grid=(t // tm,),
        in_specs=[
            pl.BlockSpec((tm, d), lambda i: (i, 0)),
            _const_spec((1, d)),
            _const_spec((d, d_ff)),
            _const_spec((d, d_ff)),
            _const_spec((d_ff, d)),
        ],
        out_specs=pl.BlockSpec((tm, d), lambda i: (i, 0)),
        out_shape=jax.ShapeDtypeStruct((t, d), F32),
        scratch_shapes=[pltpu.VMEM((tm, d_ff), BF16)],
        compiler_params=pltpu.CompilerParams(
            dimension_semantics=("arbitrary",), vmem_limit_bytes=VMEM_LIMIT),
        name="swiglu_half_step",
    )(x2, gain.reshape(1, d), wg, wu, wo)


def _hgrn2_gates(p_ref, e, lb, q_ref, k_ref, c_ref, v_ref, g_ref):
    base = e * 4 * A_DIM
    qp, fp, v, gp = (p_ref[:, base + i * A_DIM:base + (i + 1) * A_DIM] for i in range(4))
    z = jnp.exp(-jnp.abs(fp))
    r = 1.0 / (1.0 + z)
    pos = fp >= 0
    sig = jnp.where(pos, r, z * r)
    one_minus_sig = jnp.where(pos, z * r, r)
    f = jnp.maximum(lb + (1.0 - lb) * sig, F32_TINY)
    k = (1.0 - lb) * one_minus_sig
    q_ref[e] = qp * _sigmoid(qp)
    k_ref[e] = k
    c_ref[e] = jnp.log(k) * LOG2E
    v_ref[e] = v
    g_ref[e] = gp * _sigmoid(gp)
    return jnp.log(f) * LOG2E


def _hgrn2_diag(q, b2, cs, tri):
    nb = A_CHUNK // A_DIAG
    q3, b3, c3 = (t.reshape(nb, A_DIAG, LANES) for t in (q, b2, cs))
    lane3 = lax.broadcasted_iota(jnp.int32, (nb, A_DIAG, LANES), 2)
    ch = jnp.zeros((nb, A_DIAG, LANES), F32)
    for s in range(A_DIAG):
        e = jnp.exp2(b3 - c3[:, s:s + 1])
        col = jnp.sum(q3 * e, axis=-1, keepdims=True)
        ch = jnp.where(lane3 == s, col, ch)
    return jnp.where(tri >= 0, ch.reshape(A_CHUNK, LANES), 0.0).astype(BF16)


def _hgrn2_levels(a, q, k, b2, xr):
    c = A_CHUNK
    for level in A_LEVELS:
        half = level // 2
        shape4 = (c // level, 2, half, LANES)
        q4, k4, b4 = (t.reshape(shape4) for t in (q, k, b2))
        b_mid = b4[:, 0, half - 1:half]
        zero = jnp.zeros((c // level, half, LANES), F32)
        qu = q4[:, 1] * jnp.exp2(b4[:, 1] - b_mid)
        kl = k4[:, 0] * jnp.exp2(b_mid - b4[:, 0])
        ql = jnp.concatenate([zero, qu], axis=1).reshape(c, LANES).astype(BF16)
        kl = jnp.concatenate([kl, zero], axis=1).reshape(c, LANES).astype(BF16)
        al = lax.dot_general(ql, kl, NT_DIMS, preferred_element_type=F32)
        a = jnp.where(xr < half, a, al)
    return a


def _hgrn2_pair(p_ref, j, fillers, lbl_ref, og_ref, wo_ref, tril, tile, xr, tri, o_ref,
                q_ref, k_ref, b_ref, c_ref, v_ref, g_ref, cm_ref, a_ref, st_refs, oh_ref,
                *, lb_slots):
    tt = p_ref.shape[0]
    n_ch = tt // A_CHUNK
    jobs = [(e, c) for e in range(2) for c in range(n_ch)]
    rows = lambda c: slice(c * A_CHUNK, (c + 1) * A_CHUNK)

    hi_parts, lo_parts = [], []
    for e in range(2):
        lg = lbl_ref[2 * j + e]
        ex = jnp.exp(lg - jnp.max(lg, axis=0, keepdims=True))
        lb = (jnp.sum(ex[:lb_slots], axis=0, keepdims=True)
              / jnp.sum(ex, axis=0, keepdims=True))
        lf2 = _hgrn2_gates(p_ref, e, lb, q_ref, k_ref, c_ref, v_ref, g_ref)
        hi = lf2.astype(BF16)
        lo = (lf2 - hi.astype(F32)).astype(BF16)
        hi_parts += [hi[rows(c)] for c in range(n_ch)]
        lo_parts += [lo[rows(c)] for c in range(n_ch)]
    b_all = (jnp.dot(tril, jnp.concatenate(hi_parts, axis=1), preferred_element_type=F32)
             + jnp.dot(tril, jnp.concatenate(lo_parts, axis=1), preferred_element_type=F32))
    for i, (e, c) in enumerate(jobs):
        b2 = b_all[:, i * LANES:(i + 1) * LANES]
        b_ref[e, rows(c), :] = b2
        c_ref[e, rows(c), :] = b2 - c_ref[e, rows(c), :]

    fillers = list(fillers)
    for i, (e, c) in enumerate(jobs):
        cm_ref[i * A_CHUNK:(i + 1) * A_CHUNK, :] = _hgrn2_diag(
            q_ref[e, rows(c), :], b_ref[e, rows(c), :], c_ref[e, rows(c), :], tri)
        if fillers and i % 2 == 1:
            fillers.pop(0)()
    for job in fillers:
        job()
    ad_all = jnp.dot(cm_ref[...], tile, preferred_element_type=F32)

    for i, (e, c) in enumerate(jobs):
        a = _hgrn2_levels(ad_all[i * A_CHUNK:(i + 1) * A_CHUNK], q_ref[e, rows(c), :],
                          k_ref[e, rows(c), :], b_ref[e, rows(c), :], xr)
        a_ref[i] = a.astype(BF16)

    for c in range(n_ch):
        for e in range(2):
            st_ref = st_refs[e]
            q, k, b2, v = (r[e, rows(c), :] for r in (q_ref, k_ref, b_ref, v_ref))
            st = st_ref[j]
            b_last = b2[A_CHUNK - 1:A_CHUNK, :]
            q_in = (q * jnp.exp2(b2)).astype(BF16)
            o = (jnp.dot(a_ref[e * n_ch + c], v.astype(BF16), preferred_element_type=F32)
                 + lax.dot_general(q_in, st.astype(BF16), NT_DIMS, preferred_element_type=F32))
            k_out = (k * jnp.exp2(b_last - b2)).astype(BF16)
            st_ref[j] = st * jnp.exp2(b_last) + jnp.dot(v.T.astype(BF16), k_out,
                                                         preferred_element_type=F32)
            o = _rms_rows(o, og_ref[...]) * g_ref[e, rows(c), :]
            oh_ref[rows(c), e * A_DIM:(e + 1) * A_DIM] = o.astype(BF16)
    o_ref[0] += jnp.dot(oh_ref[...], wo_ref[j], preferred_element_type=F32)


def _hgrn2_kernel(x_ref, gain_ref, w_ref, lbl_ref, og_ref, wo_ref, tril_ref, tile_ref, o_ref,
                  h_ref, pa_ref, pb_ref, q_ref, k_ref, b_ref, c_ref, v_ref, g_ref, cm_ref,
                  a_ref, st0_ref, st1_ref, oh_ref, *, lb_slots):
    @pl.when(pl.program_id(1) == 0)
    def _():
        st0_ref[...] = jnp.zeros_like(st0_ref)
        st1_ref[...] = jnp.zeros_like(st1_ref)

    x = x_ref[0]
    o_ref[0] = x
    h_ref[...] = _rms_rows(x, gain_ref[...]).astype(BF16)
    tril = tril_ref[...]
    tile = tile_ref[...]
    row = lax.broadcasted_iota(jnp.int32, (A_CHUNK, LANES), 0)
    lane = lax.broadcasted_iota(jnp.int32, (A_CHUNK, LANES), 1)
    xr = jnp.bitwise_xor(row, lane)
    tri = jnp.bitwise_and(row, A_DIAG - 1) - lane
    n_pairs = w_ref.shape[0]
    width = w_ref.shape[2]

    def projection_jobs(dst_ref, pair):
        def job(c0):
            def run():
                dst_ref[:, c0:c0 + MXU_COLS] = jnp.dot(
                    h_ref[...], w_ref[pair, :, c0:c0 + MXU_COLS], preferred_element_type=F32)
            return run
        return [job(c0) for c0 in range(0, width, MXU_COLS)]

    common = dict(lbl_ref=lbl_ref, og_ref=og_ref, wo_ref=wo_ref, tril=tril, tile=tile, xr=xr,
                  tri=tri, o_ref=o_ref, q_ref=q_ref, k_ref=k_ref, b_ref=b_ref, c_ref=c_ref,
                  v_ref=v_ref, g_ref=g_ref, cm_ref=cm_ref, a_ref=a_ref,
                  st_refs=(st0_ref, st1_ref), oh_ref=oh_ref, lb_slots=lb_slots)

    for job in projection_jobs(pa_ref, 0):
        job()

    def two_pairs(m, carry):
        _hgrn2_pair(pa_ref, 2 * m, projection_jobs(pb_ref, 2 * m + 1), **common)
        nxt = jnp.minimum(2 * m + 2, n_pairs - 1)
        _hgrn2_pair(pb_ref, 2 * m + 1, projection_jobs(pa_ref, nxt), **common)
        return carry

    lax.fori_loop(0, n_pairs // 2, two_pairs, 0)


def _hgrn2_layer(x, gain, w_in, lb_logits, layer, out_gain, w_out, *, tt=512):
    bsz, seq, d = x.shape
    n_slots = lb_logits.shape[0]
    n_pairs = A_HEADS // 2
    n_jobs = 2 * (tt // A_CHUNK)
    w_pairs = (w_in.reshape(d, 4, n_pairs, 2, A_DIM).transpose(2, 0, 3, 1, 4)
               .reshape(n_pairs, d, 8 * A_DIM).astype(BF16))
    wo_pairs = w_out.reshape(n_pairs, 2 * A_DIM, d).astype(BF16)
    lbl = lb_logits.reshape(n_slots, A_HEADS, A_DIM).transpose(1, 0, 2)
    tril = jnp.asarray(np.tril(np.ones((A_CHUNK, A_CHUNK), np.float32)), BF16)
    idx = np.arange(LANES)
    tile = jnp.asarray((idx[:, None] == idx[None, :] % A_DIAG).astype(np.float32), BF16)
    head_tile = lambda dtype: pltpu.VMEM((2, tt, A_DIM), dtype)
    return pl.pallas_call(
        functools.partial(_hgrn2_kernel, lb_slots=layer + 1),
        grid=(bsz, seq // tt),
        in_specs=[
            pl.BlockSpec((1, tt, d), lambda b, i: (b, i, 0)),
            _const_spec((1, d)),
            _const_spec((n_pairs, d, 8 * A_DIM)),
            _const_spec((A_HEADS, n_slots, A_DIM)),
            _const_spec((1, A_DIM)),
            _const_spec((n_pairs, 2 * A_DIM, d)),
            _const_spec((A_CHUNK, A_CHUNK)),
            _const_spec((LANES, LANES)),
        ],
        out_specs=pl.BlockSpec((1, tt, d), lambda b, i: (b, i, 0)),
        out_shape=jax.ShapeDtypeStruct((bsz, seq, d), F32),
        scratch_shapes=[
            pltpu.VMEM((tt, d), BF16),
            pltpu.VMEM((tt, 8 * A_DIM), F32),
            pltpu.VMEM((tt, 8 * A_DIM), F32),
            head_tile(F32), head_tile(F32), head_tile(F32),
            head_tile(F32), head_tile(F32), head_tile(F32),
            pltpu.VMEM((n_jobs * A_CHUNK, LANES), BF16),
            pltpu.VMEM((n_jobs, A_CHUNK, LANES), BF16),
            pltpu.VMEM((n_pairs, A_DIM, A_DIM), F32),
            pltpu.VMEM((n_pairs, A_DIM, A_DIM), F32),
            pltpu.VMEM((tt, 2 * A_DIM), BF16),
        ],
        compiler_params=pltpu.CompilerParams(
            dimension_semantics=("arbitrary", "arbitrary"), vmem_limit_bytes=VMEM_LIMIT),
        name="hgrn2_layer",
    )(x, gain.reshape(1, d), w_pairs, lbl, out_gain.reshape(1, A_DIM), wo_pairs, tril, tile)


def _proj_kernel(x_ref, gain_ref, w_ref, hg_ref, *out_refs, n_normed, scale):
    h = _rms_rows(x_ref[...], gain_ref[...]).astype(BF16)
    width = out_refs[0].shape[1]
    for j, o_ref in enumerate(out_refs):
        for c0 in range(0, width, MXU_COLS):
            col = j * width + c0
            t2 = jnp.dot(h, w_ref[:, col:col + MXU_COLS], preferred_element_type=F32)
            for c1 in range(c0, c0 + MXU_COLS, LANES):
                t = t2[:, c1 - c0:c1 - c0 + LANES]
                if j < n_normed:
                    t = _head_rms_pair(t, hg_ref[j:j + 1, c1:c1 + LANES]) * scale
                o_ref[:, c1:c1 + LANES] = t.astype(BF16)


def _project(x2, gain, w, head_gain, *, n_out, n_normed, scale, tm=512):
    t, d = x2.shape
    width = w.shape[1] // n_out
    hg = jnp.tile(head_gain, (1, B_HEADS))
    outs = pl.pallas_call(
        functools.partial(_proj_kernel, n_normed=n_normed, scale=scale),
        grid=(t // tm,),
        in_specs=[
            pl.BlockSpec((tm, d), lambda i: (i, 0)),
            _const_spec((1, d)),
            _const_spec((d, n_out * width)),
            _const_spec((n_normed, width)),
        ],
        out_specs=[pl.BlockSpec((tm, width), lambda i: (i, 0)) for _ in range(n_out)],
        out_shape=[jax.ShapeDtypeStruct((t, width), BF16) for _ in range(n_out)],
        compiler_params=pltpu.CompilerParams(
            dimension_semantics=("arbitrary",), vmem_limit_bytes=VMEM_LIMIT),
        name="norm_projection",
    )(x2, gain.reshape(1, d), w.astype(BF16), hg)
    return outs


def _t5_bucket(dist):
    dist = np.asarray(dist, np.int32)
    max_exact = NUM_BUCKETS // 2
    large = max_exact + (np.log(np.maximum(dist, 1) / max_exact)
                         / math.log(MAX_DISTANCE / max_exact) * (NUM_BUCKETS - max_exact)).astype(np.int32)
    large = np.minimum(large, NUM_BUCKETS - 1)
    return np.where(dist < max_exact, dist, large).astype(np.int32)


def _band_bias(rel_bias, group):
    dil = B_DILATIONS[group]
    wd = B_WINDOWS[group] // dil
    i = np.arange(B_BLOCK)[:, None]
    j = np.arange(2 * B_BLOCK)[None, :]
    dist = i + wd - j
    band = (dist >= 0) & (dist <= wd)
    bucket = _t5_bucket(np.clip(dist, 0, wd) * dil)
    ids = jnp.asarray(np.where(band, bucket, -1), jnp.int32)
    onehot = (ids[None] == jnp.arange(NUM_BUCKETS, dtype=jnp.int32)[:, None, None]).astype(F32)
    table = rel_bias[:, group * B_HEADS:(group + 1) * B_HEADS].astype(F32)
    bias = jnp.einsum("bh,bij->hij", table, onehot,
                      precision=lax.Precision.HIGHEST)
    bias = bias + jnp.asarray(np.where(band, 0.0, NEG), F32)[None]
    return bias.reshape(B_HEADS // 2, 2 * B_BLOCK, 2 * B_BLOCK)


def _attn_kernel(q_ref, kp_ref, kc_ref, vp_ref, vc_ref, bias_ref, o_ref, lse_ref):
    nq = B_BLOCK
    first = pl.program_id(2) == 0
    lo = _half_lane_mask((nq, LANES))
    lane = lax.broadcasted_iota(jnp.int32, (nq, LANES), 1)
    key_is_prev = lax.broadcasted_iota(jnp.int32, (2 * nq, 2 * nq), 1) < nq
    prev_pad = jnp.where(jnp.logical_and(first, key_is_prev), NEG, 0.0)
    ones = jnp.ones((2 * nq, LANES), BF16)
    lse_tile = jnp.zeros((nq, LANES), F32)
    for pair in range(B_HEADS // 2):
        sl = slice(pair * LANES, (pair + 1) * LANES)
        qp = q_ref[0, :, sl]
        zero = jnp.zeros_like(qp)
        qq = jnp.concatenate([jnp.where(lo, qp, zero), jnp.where(lo, zero, qp)], axis=0)
        kk = jnp.concatenate([kp_ref[0, :, sl], kc_ref[0, :, sl]], axis=0)
        vv = jnp.concatenate([vp_ref[0, :, sl], vc_ref[0, :, sl]], axis=0)
        s = lax.dot_general(qq, kk, NT_DIMS, preferred_element_type=F32)
        s = s + bias_ref[pair] + prev_pad
        m = jnp.max(s, axis=-1, keepdims=True)
        p = jnp.exp(s - m).astype(BF16)
        ov = jnp.dot(p, jnp.concatenate([vv, ones], axis=1), preferred_element_type=F32)
        l = ov[:, LANES:]
        on = ov[:, :LANES] / l
        o_ref[0, :, sl] = jnp.where(lo, on[:nq], on[nq:]).astype(BF16)
        lse = m + jnp.log(l)
        lse_tile = jnp.where(lane == 2 * pair, lse[:nq], lse_tile)
        lse_tile = jnp.where(lane == 2 * pair + 1, lse[nq:], lse_tile)
    lse_ref[0] = lse_tile


def _attn_group(q, k, v, rel_bias, group):
    bsz, seq, width = q.shape
    dil = B_DILATIONS[group]
    n_sub = seq // dil
    n_tiles = n_sub // B_BLOCK
    view = lambda a: a.reshape(bsz, n_sub, dil * a.shape[-1])
    cur = lambda w: pl.BlockSpec((1, B_BLOCK, w), lambda b, r, i: (b, i, r))
    prev = lambda w: pl.BlockSpec((1, B_BLOCK, w), lambda b, r, i: (b, jnp.maximum(i - 1, 0), r))
    o, lse = pl.pallas_call(
        _attn_kernel,
        grid=(bsz, dil, n_tiles),
        in_specs=[cur(width), prev(width), cur(width), prev(width), cur(width),
                  _const_spec((B_HEADS // 2, 2 * B_BLOCK, 2 * B_BLOCK))],
        out_specs=[cur(width), cur(LANES)],
        out_shape=[jax.ShapeDtypeStruct((bsz, n_sub, dil * width), BF16),
                   jax.ShapeDtypeStruct((bsz, n_sub, dil * LANES), F32)],
        compiler_params=pltpu.CompilerParams(
            dimension_semantics=("arbitrary", "arbitrary", "arbitrary"),
            vmem_limit_bytes=VMEM_LIMIT),
        name=f"dilated_attention_g{group}",
    )(view(q), view(k), view(k), view(v), view(v), _band_bias(rel_bias, group))
    return o.reshape(bsz, seq, width), lse.reshape(bsz, seq, LANES)


def _merge_kernel(x_ref, o0_ref, o1_ref, o2_ref, l0_ref, l1_ref, l2_ref, ex_ref, wo_ref, out_ref):
    lses = [l0_ref[...], l1_ref[...], l2_ref[...]]
    m = jnp.maximum(jnp.maximum(lses[0], lses[1]), lses[2])
    es = [jnp.exp(l - m) for l in lses]
    inv = 1.0 / (es[0] + es[1] + es[2])
    acc = None
    for e, o_ref in zip(es, (o0_ref, o1_ref, o2_ref)):
        w = jnp.dot((e * inv).astype(BF16), ex_ref[...], preferred_element_type=F32)
        term = w * o_ref[...].astype(F32)
        acc = term if acc is None else acc + term
    out_ref[...] = x_ref[...] + jnp.dot(acc.astype(BF16), wo_ref[...], preferred_element_type=F32)


def _merge_out(x2, os, lses, w_o, *, tm=512):
    t, d = x2.shape
    width = os[0].shape[-1]
    expand = np.zeros((LANES, width), np.float32)
    for h in range(B_HEADS):
        expand[h, h * B_HEAD_DIM:(h + 1) * B_HEAD_DIM] = 1.0
    row = lambda w: pl.BlockSpec((tm, w), lambda i: (i, 0))
    return pl.pallas_call(
        _merge_kernel,
        grid=(t // tm,),
        in_specs=[row(d), row(width), row(width), row(width), row(LANES), row(LANES), row(LANES),
                  _const_spec((LANES, width)), _const_spec((width, d))],
        out_specs=row(d),
        out_shape=jax.ShapeDtypeStruct((t, d), F32),
        compiler_params=pltpu.CompilerParams(
            dimension_semantics=("arbitrary",), vmem_limit_bytes=VMEM_LIMIT),
        name="merge_out_projection",
    )(x2, *[o.reshape(t, width) for o in os], *[l.reshape(t, LANES) for l in lses],
      jnp.asarray(expand, BF16), w_o.astype(BF16))


def kernel(x, norm_gain, ffn_w_in, ffn_w_out, a_w_in, a_lb_logits, a_out_gain, a_w_out,
           kv_norm, w_kv, k_gain, b_w_q, b_q_gain, b_w_o, rel_bias):
    bsz, seq, d = x.shape
    depth = norm_gain.shape[0]
    n_a = a_w_in.shape[0]
    t = bsz * seq
    ks = vs = None
    for layer in range(depth):
        if layer == n_a:
            outs = _project(x.reshape(t, d), kv_norm, w_kv, k_gain,
                            n_out=2 * B_GROUPS, n_normed=B_GROUPS, scale=1.0)
            ks = [o.reshape(bsz, seq, -1) for o in outs[:B_GROUPS]]
            vs = [o.reshape(bsz, seq, -1) for o in outs[B_GROUPS:]]
        x = _ffn(x.reshape(t, d), norm_gain[layer, 0], ffn_w_in[layer, 0],
                 ffn_w_out[layer, 0]).reshape(bsz, seq, d)
        if layer < n_a:
            x = _hgrn2_layer(x, norm_gain[layer, 1], a_w_in[layer], a_lb_logits, layer,
                             a_out_gain[layer], a_w_out[layer])
        else:
            j = layer - n_a
            qs = _project(x.reshape(t, d), norm_gain[layer, 1], b_w_q[j], b_q_gain[j],
                          n_out=B_GROUPS, n_normed=B_GROUPS, scale=B_HEAD_DIM ** -0.5)
            os, lses = [], []
            for g in range(B_GROUPS):
                o, lse = _attn_group(qs[g].reshape(bsz, seq, -1), ks[g], vs[g], rel_bias, g)
                os.append(o)
                lses.append(lse)
            x = _merge_out(x.reshape(t, d), os, lses, b_w_o[j]).reshape(bsz, seq, d)
        x = _ffn(x.reshape(t, d), norm_gain[layer, 2], ffn_w_in[layer, 1],
                 ffn_w_out[layer, 1]).reshape(bsz, seq, d)
    return x
```

```python
import functools
import math

import jax
import jax.numpy as jnp
import numpy as np
from jax import lax
from jax.experimental import pallas as pl
from jax.experimental.pallas import tpu as pltpu

F32 = jnp.float32
BF16 = jnp.bfloat16

EPS = 1e-6
LOG2E = math.log2(math.e)
F32_TINY = float(np.finfo(np.float32).tiny)
LANES = 128
SUBLANES = 8
MXU_COLS = 256
VMEM_LIMIT = 56 * 1024 * 1024

A_HEADS = 8
A_DIM = 128
A_CHUNK = 128
A_DIAG = SUBLANES
A_LEVELS = (16, 32, 64, 128)

B_WINDOWS = (128, 512, 2048)
B_DILATIONS = (1, 4, 16)
B_GROUPS = 3
B_HEADS = 16
B_HEAD_DIM = 64
B_BLOCK = 128
NUM_BUCKETS = 32
MAX_DISTANCE = 2048
NEG = -1e30

NT_DIMS = (((1,), (1,)), ((), ()))


def _const_spec(shape):
    nd = len(shape)
    return pl.BlockSpec(shape, lambda *_: (0,) * nd, pipeline_mode=pl.Buffered(1))


def _sigmoid(x):
    return 1.0 / (1.0 + jnp.exp(-x))


def _rms_rows(x, gain):
    ms = jnp.mean(x * x, axis=-1, keepdims=True)
    return x * lax.rsqrt(ms + EPS) * gain


def _half_lane_mask(shape):
    return lax.broadcasted_iota(jnp.int32, shape, len(shape) - 1) < (LANES // 2)


def _head_rms_pair(t, gain_row):
    lo = _half_lane_mask(t.shape)
    sq = t * t
    s_lo = jnp.sum(jnp.where(lo, sq, 0.0), axis=-1, keepdims=True)
    s_hi = jnp.sum(jnp.where(lo, 0.0, sq), axis=-1, keepdims=True)
    ms = jnp.where(lo, s_lo, s_hi) * (1.0 / B_HEAD_DIM)
    return t * lax.rsqrt(ms + EPS) * gain_row


def _ffn_kernel(x_ref, gain_ref, wg_ref, wu_ref, wo_ref, o_ref, act_ref, *, ff_chunk):
    x = x_ref[...]
    h = _rms_rows(x, gain_ref[...]).astype(BF16)
    d_ff = wg_ref.shape[1]
    for c0 in range(0, d_ff, ff_chunk):
        sl = slice(c0, c0 + ff_chunk)
        g = jnp.dot(h, wg_ref[:, sl], preferred_element_type=F32)
        u = jnp.dot(h, wu_ref[:, sl], preferred_element_type=F32)
        act_ref[:, sl] = (g * _sigmoid(g) * u).astype(BF16)
    y = jnp.dot(act_ref[...], wo_ref[...], preferred_element_type=F32)
    o_ref[...] = x + 0.5 * y


def _ffn(x2, gain, w_in, w_out, *, tm=512, ff_chunk=256):
    t, d = x2.shape
    d_ff = w_out.shape[0]
    wg = w_in[:, :d_ff].astype(BF16)
    wu = w_in[:, d_ff:].astype(BF16)
    wo = w_out.astype(BF16)
    return pl.pallas_call(
        functools.partial(_ffn_kernel, ff_chunk=ff_chunk),
        grid=(t // tm,),
        in_specs=[
            pl.BlockSpec((tm, d), lambda i: (i, 0)),
            _const_spec((1, d)),
            _const_spec((d, d_ff)),
            _const_spec((d, d_ff)),
            _const_spec((d_ff, d)),
        ],
        out_specs=pl.BlockSpec((tm, d), lambda i: (i, 0)),
        out_shape=jax.ShapeDtypeStruct((t, d), F32),
        scratch_shapes=[pltpu.VMEM((tm, d_ff), BF16)],
        compiler_params=pltpu.CompilerParams(
            dimension_semantics=("arbitrary",), vmem_limit_bytes=VMEM_LIMIT),
        name="swiglu_half_step",
    )(x2, gain.reshape(1, d), wg, wu, wo)


def _hgrn2_gates(p_ref, e, lb, q_ref, k_ref, c_ref, v_ref, g_ref):
    base = e * 4 * A_DIM
    qp, fp, v, gp = (p_ref[:, base + i * A_DIM:base + (i + 1) * A_DIM] for i in range(4))
    z = jnp.exp(-jnp.abs(fp))
    r = 1.0 / (1.0 + z)
    pos = fp >= 0
    sig = jnp.where(pos, r, z * r)
    one_minus_sig = jnp.where(pos, z * r, r)
    f = jnp.maximum(lb + (1.0 - lb) * sig, F32_TINY)
    k = (1.0 - lb) * one_minus_sig
    q_ref[e] = qp * _sigmoid(qp)
    k_ref[e] = k
    c_ref[e] = jnp.log(k) * LOG2E
    v_ref[e] = v
    g_ref[e] = gp * _sigmoid(gp)
    return jnp.log(f) * LOG2E


def _hgrn2_diag(q, b2, cs, tri):
    nb = A_CHUNK // A_DIAG
    q3, b3, c3 = (t.reshape(nb, A_DIAG, LANES) for t in (q, b2, cs))
    lane3 = lax.broadcasted_iota(jnp.int32, (nb, A_DIAG, LANES), 2)
    ch = jnp.zeros((nb, A_DIAG, LANES), F32)
    for s in range(A_DIAG):
        e = jnp.exp2(b3 - c3[:, s:s + 1])
        col = jnp.sum(q3 * e, axis=-1, keepdims=True)
        ch = jnp.where(lane3 == s, col, ch)
    return jnp.where(tri >= 0, ch.reshape(A_CHUNK, LANES), 0.0).astype(BF16)


def _hgrn2_levels(a, q, k, b2, xr):
    c = A_CHUNK
    for level in A_LEVELS:
        half = level // 2
        shape4 = (c // level, 2, half, LANES)
        q4, k4, b4 = (t.reshape(shape4) for t in (q, k, b2))
        b_mid = b4[:, 0, half - 1:half]
        zero = jnp.zeros((c // level, half, LANES), F32)
        qu = q4[:, 1] * jnp.exp2(b4[:, 1] - b_mid)
        kl = k4[:, 0] * jnp.exp2(b_mid - b4[:, 0])
        ql = jnp.concatenate([zero, qu], axis=1).reshape(c, LANES).astype(BF16)
        kl = jnp.concatenate([kl, zero], axis=1).reshape(c, LANES).astype(BF16)
        al = lax.dot_general(ql, kl, NT_DIMS, preferred_element_type=F32)
        a = jnp.where(xr < half, a, al)
    return a


def _hgrn2_pair(p_ref, j, fillers, lbl_ref, og_ref, wo_ref, tril, tile, xr, tri, o_ref,
                q_ref, k_ref, b_ref, c_ref, v_ref, g_ref, cm_ref, a_ref, st_refs, oh_ref,
                *, lb_slots):
    tt = p_ref.shape[0]
    n_ch = tt // A_CHUNK
    jobs = [(e, c) for e in range(2) for c in range(n_ch)]
    rows = lambda c: slice(c * A_CHUNK, (c + 1) * A_CHUNK)

    hi_parts, lo_parts = [], []
    for e in range(2):
        lg = lbl_ref[2 * j + e]
        ex = jnp.exp(lg - jnp.max(lg, axis=0, keepdims=True))
        lb = (jnp.sum(ex[:lb_slots], axis=0, keepdims=True)
              / jnp.sum(ex, axis=0, keepdims=True))
        lf2 = _hgrn2_gates(p_ref, e, lb, q_ref, k_ref, c_ref, v_ref, g_ref)
        hi = lf2.astype(BF16)
        lo = (lf2 - hi.astype(F32)).astype(BF16)
        hi_parts += [hi[rows(c)] for c in range(n_ch)]
        lo_parts += [lo[rows(c)] for c in range(n_ch)]
    b_all = (jnp.dot(tril, jnp.concatenate(hi_parts, axis=1), preferred_element_type=F32)
             + jnp.dot(tril, jnp.concatenate(lo_parts, axis=1), preferred_element_type=F32))
    for i, (e, c) in enumerate(jobs):
        b2 = b_all[:, i * LANES:(i + 1) * LANES]
        b_ref[e, rows(c), :] = b2
        c_ref[e, rows(c), :] = b2 - c_ref[e, rows(c), :]

    fillers = list(fillers)
    for i, (e, c) in enumerate(jobs):
        cm_ref[i * A_CHUNK:(i + 1) * A_CHUNK, :] = _hgrn2_diag(
            q_ref[e, rows(c), :], b_ref[e, rows(c), :], c_ref[e, rows(c), :], tri)
        if fillers and i % 2 == 1:
            fillers.pop(0)()
    for job in fillers:
        job()
    ad_all = jnp.dot(cm_ref[...], tile, preferred_element_type=F32)

    for i, (e, c) in enumerate(jobs):
        a = _hgrn2_levels(ad_all[i * A_CHUNK:(i + 1) * A_CHUNK], q_ref[e, rows(c), :],
                          k_ref[e, rows(c), :], b_ref[e, rows(c), :], xr)
        a_ref[i] = a.astype(BF16)

    for c in range(n_ch):
        for e in range(2):
            st_ref = st_refs[e]
            q, k, b2, v = (r[e, rows(c), :] for r in (q_ref, k_ref, b_ref, v_ref))
            st = st_ref[j]
            b_last = b2[A_CHUNK - 1:A_CHUNK, :]
            q_in = (q * jnp.exp2(b2)).astype(BF16)
            o = (jnp.dot(a_ref[e * n_ch + c], v.astype(BF16), preferred_element_type=F32)
                 + lax.dot_general(q_in, st.astype(BF16), NT_DIMS, preferred_element_type=F32))
            k_out = (k * jnp.exp2(b_last - b2)).astype(BF16)
            st_ref[j] = st * jnp.exp2(b_last) + jnp.dot(v.T.astype(BF16), k_out,
                                                         preferred_element_type=F32)
            o = _rms_rows(o, og_ref[...]) * g_ref[e, rows(c), :]
            oh_ref[rows(c), e * A_DIM:(e + 1) * A_DIM] = o.astype(BF16)
    o_ref[0] += jnp.dot(oh_ref[...], wo_ref[j], preferred_element_type=F32)


def _hgrn2_kernel(x_ref, gain_ref, w_ref, lbl_ref, og_ref, wo_ref, tril_ref, tile_ref, o_ref,
                  h_ref, pa_ref, pb_ref, q_ref, k_ref, b_ref, c_ref, v_ref, g_ref, cm_ref,
                  a_ref, st0_ref, st1_ref, oh_ref, *, lb_slots):
    @pl.when(pl.program_id(1) == 0)
    def _():
        st0_ref[...] = jnp.zeros_like(st0_ref)
        st1_ref[...] = jnp.zeros_like(st1_ref)

    x = x_ref[0]
    o_ref[0] = x
    h_ref[...] = _rms_rows(x, gain_ref[...]).astype(BF16)
    tril = tril_ref[...]
    tile = tile_ref[...]
    row = lax.broadcasted_iota(jnp.int32, (A_CHUNK, LANES), 0)
    lane = lax.broadcasted_iota(jnp.int32, (A_CHUNK, LANES), 1)
    xr = jnp.bitwise_xor(row, lane)
    tri = jnp.bitwise_and(row, A_DIAG - 1) - lane
    n_pairs = w_ref.shape[0]
    width = w_ref.shape[2]

    def projection_jobs(dst_ref, pair):
        def job(c0):
            def run():
                dst_ref[:, c0:c0 + MXU_COLS] = jnp.dot(
                    h_ref[...], w_ref[pair, :, c0:c0 + MXU_COLS], preferred_element_type=F32)
            return run
        return [job(c0) for c0 in range(0, width, MXU_COLS)]

    common = dict(lbl_ref=lbl_ref, og_ref=og_ref, wo_ref=wo_ref, tril=tril, tile=tile, xr=xr,
                  tri=tri, o_ref=o_ref, q_ref=q_ref, k_ref=k_ref, b_ref=b_ref, c_ref=c_ref,
                  v_ref=v_ref, g_ref=g_ref, cm_ref=cm_ref, a_ref=a_ref,
                  st_refs=(st0_ref, st1_ref), oh_ref=oh_ref, lb_slots=lb_slots)

    for job in projection_jobs(pa_ref, 0):
        job()

    def two_pairs(m, carry):
        _hgrn2_pair(pa_ref, 2 * m, projection_jobs(pb_ref, 2 * m + 1), **common)
        nxt = jnp.minimum(2 * m + 2, n_pairs - 1)
        _hgrn2_pair(pb_ref, 2 * m + 1, projection_jobs(pa_ref, nxt), **common)
        return carry

    lax.fori_loop(0, n_pairs // 2, two_pairs, 0)


def _hgrn2_layer(x, gain, w_in, lb_logits, layer, out_gain, w_out, *, tt=512):
    bsz, seq, d = x.shape
    n_slots = lb_logits.shape[0]
    n_pairs = A_HEADS // 2
    n_jobs = 2 * (tt // A_CHUNK)
    w_pairs = (w_in.reshape(d, 4, n_pairs, 2, A_DIM).transpose(2, 0, 3, 1, 4)
               .reshape(n_pairs, d, 8 * A_DIM).astype(BF16))
    wo_pairs = w_out.reshape(n_pairs, 2 * A_DIM, d).astype(BF16)
    lbl = lb_logits.reshape(n_slots, A_HEADS, A_DIM).transpose(1, 0, 2)
    tril = jnp.asarray(np.tril(np.ones((A_CHUNK, A_CHUNK), np.float32)), BF16)
    idx = np.arange(LANES)
    tile = jnp.asarray((idx[:, None] == idx[None, :] % A_DIAG).astype(np.float32), BF16)
    head_tile = lambda dtype: pltpu.VMEM((2, tt, A_DIM), dtype)
    return pl.pallas_call(
        functools.partial(_hgrn2_kernel, lb_slots=layer + 1),
        grid=(bsz, seq // tt),
        in_specs=[
            pl.BlockSpec((1, tt, d), lambda b, i: (b, i, 0)),
            _const_spec((1, d)),
            _const_spec((n_pairs, d, 8 * A_DIM)),
            _const_spec((A_HEADS, n_slots, A_DIM)),
            _const_spec((1, A_DIM)),
            _const_spec((n_pairs, 2 * A_DIM, d)),
            _const_spec((A_CHUNK, A_CHUNK)),
            _const_spec((LANES, LANES)),
        ],
        out_specs=pl.BlockSpec((1, tt, d), lambda b, i: (b, i, 0)),
        out_shape=jax.ShapeDtypeStruct((bsz, seq, d), F32),
        scratch_shapes=[
            pltpu.VMEM((tt, d), BF16),
            pltpu.VMEM((tt, 8 * A_DIM), F32),
            pltpu.VMEM((tt, 8 * A_DIM), F32),
            head_tile(F32), head_tile(F32), head_tile(F32),
            head_tile(F32), head_tile(F32), head_tile(F32),
            pltpu.VMEM((n_jobs * A_CHUNK, LANES), BF16),
            pltpu.VMEM((n_jobs, A_CHUNK, LANES), BF16),
            pltpu.VMEM((n_pairs, A_DIM, A_DIM), F32),
            pltpu.VMEM((n_pairs, A_DIM, A_DIM), F32),
            pltpu.VMEM((tt, 2 * A_DIM), BF16),
        ],
        compiler_params=pltpu.CompilerParams(
            dimension_semantics=("arbitrary", "arbitrary"), vmem_limit_bytes=VMEM_LIMIT),
        name="hgrn2_layer",
    )(x, gain.reshape(1, d), w_pairs, lbl, out_gain.reshape(1, A_DIM), wo_pairs, tril, tile)


def _proj_kernel(x_ref, gain_ref, w_ref, hg_ref, *refs, n_normed, scale):
    out_refs, hs_ref = refs[:-1], refs[-1]
    tm = x_ref.shape[1]
    width = out_refs[0].shape[-1]
    hf = _rms_rows(x_ref[0], gain_ref[...])
    n_slabs = hs_ref.shape[0]
    for s in range(n_slabs):
        hs_ref[s] = hf[:, s * LANES:(s + 1) * LANES]
    lhs = {}
    for dil in sorted(set(B_DILATIONS)):
        if dil == 1:
            lhs[dil] = hf.astype(BF16)
            continue
        slabs = [jnp.concatenate([hs_ref[s, pl.ds(r, tm // dil, stride=dil), :]
                                  for r in range(dil)], axis=0).astype(BF16)
                 for s in range(n_slabs)]
        lhs[dil] = jnp.concatenate(slabs, axis=1)
    for j, o_ref in enumerate(out_refs):
        kind, g = divmod(j, B_GROUPS)
        dil = B_DILATIONS[g]
        for c0 in range(0, width, MXU_COLS):
            col = j * width + c0
            t2 = jnp.dot(lhs[dil], w_ref[:, col:col + MXU_COLS], preferred_element_type=F32)
            for c1 in range(c0, c0 + MXU_COLS, LANES):
                t = t2[:, c1 - c0:c1 - c0 + LANES]
                if kind < n_normed:
                    t = _head_rms_pair(t, hg_ref[g:g + 1, c1:c1 + LANES]) * scale
                o_ref[0, :, :, c1:c1 + LANES] = t.astype(BF16).reshape(dil, tm // dil, LANES)


def _project(x, gain, w, head_gain, *, n_kinds, n_normed, scale, tm=512):
    bsz, seq, d = x.shape
    n_out = n_kinds * B_GROUPS
    width = w.shape[1] // n_out
    hg = jnp.tile(head_gain, (1, B_HEADS))
    dils = [B_DILATIONS[j % B_GROUPS] for j in range(n_out)]
    outs = pl.pallas_call(
        functools.partial(_proj_kernel, n_normed=n_normed, scale=scale),
        grid=(bsz, seq // tm),
        in_specs=[
            pl.BlockSpec((1, tm, d), lambda b, i: (b, i, 0)),
            _const_spec((1, d)),
            _const_spec((d, n_out * width)),
            _const_spec((B_GROUPS, width)),
        ],
        out_specs=[pl.BlockSpec((1, dil, tm // dil, width), lambda b, i: (b, 0, i, 0))
                   for dil in dils],
        out_shape=[jax.ShapeDtypeStruct((bsz, dil, seq // dil, width), BF16) for dil in dils],
        scratch_shapes=[pltpu.VMEM((d // LANES, tm, LANES), F32)],
        compiler_params=pltpu.CompilerParams(
            dimension_semantics=("arbitrary", "arbitrary"), vmem_limit_bytes=VMEM_LIMIT),
        name="norm_projection",
    )(x, gain.reshape(1, d), w.astype(BF16), hg)
    return outs


def _t5_bucket(dist):
    dist = np.asarray(dist, np.int32)
    max_exact = NUM_BUCKETS // 2
    large = max_exact + (np.log(np.maximum(dist, 1) / max_exact)
                         / math.log(MAX_DISTANCE / max_exact) * (NUM_BUCKETS - max_exact)).astype(np.int32)
    large = np.minimum(large, NUM_BUCKETS - 1)
    return np.where(dist < max_exact, dist, large).astype(np.int32)


def _band_bias(rel_bias, group):
    dil = B_DILATIONS[group]
    wd = B_WINDOWS[group] // dil
    i = np.arange(B_BLOCK)[:, None]
    j = np.arange(2 * B_BLOCK)[None, :]
    dist = i + wd - j
    band = (dist >= 0) & (dist <= wd)
    bucket = _t5_bucket(np.clip(dist, 0, wd) * dil)
    ids = jnp.asarray(np.where(band, bucket, -1), jnp.int32)
    onehot = (ids[None] == jnp.arange(NUM_BUCKETS, dtype=jnp.int32)[:, None, None]).astype(F32)
    table = rel_bias[:, group * B_HEADS:(group + 1) * B_HEADS].astype(F32)
    bias = jnp.einsum("bh,bij->hij", table, onehot,
                      precision=lax.Precision.HIGHEST) * LOG2E
    pair_shape = (B_HEADS // 2, 2 * B_BLOCK, 2 * B_BLOCK)
    tables = []
    for keep in (band & (j >= B_BLOCK), band):
        tables.append((bias + jnp.asarray(np.where(keep, 0.0, NEG), F32)[None]).reshape(pair_shape))
    return jnp.stack(tables)


def _stat_lane(head):
    return (head % 2) * (LANES // 2) + head // 2


def _attn_kernel(q_ref, kp_ref, kc_ref, vp_ref, vc_ref, bias_ref, o_ref, m_ref, l_ref):
    nq = B_BLOCK
    half = LANES // 2
    n_tiles = q_ref.shape[2] // nq
    n_pairs = B_HEADS // 2
    first_table = jnp.where(pl.program_id(2) == 0, 0, 1)
    lo_q = _half_lane_mask((nq, LANES))
    lo_k = _half_lane_mask((2 * nq, LANES))
    lane = lax.broadcasted_iota(jnp.int32, (nq, LANES), 1)
    stat_lane = jnp.bitwise_and(lane, half - 1)
    ones_lo = jnp.where(lo_k, 1.0, 0.0).astype(BF16)
    ones_hi = jnp.where(lo_k, 0.0, 1.0).astype(BF16)

    def keys_values(prev_ref, cur_ref, t, sl):
        if t == 0:
            return jnp.concatenate([prev_ref[0, 0, :, sl], cur_ref[0, 0, :nq, sl]], axis=0)
        return cur_ref[0, 0, (t - 1) * nq:(t + 1) * nq, sl]

    def scores(job):
        t, pair = divmod(job, n_pairs)
        sl = slice(pair * LANES, (pair + 1) * LANES)
        qp = q_ref[0, 0, t * nq:(t + 1) * nq, sl]
        kk = keys_values(kp_ref, kc_ref, t, sl)
        zq = jnp.zeros_like(qp)
        qq = jnp.concatenate([jnp.where(lo_q, qp, zq), jnp.where(lo_q, zq, qp)], axis=0)
        table = first_table if t == 0 else 1
        return lax.dot_general(qq, kk, NT_DIMS, preferred_element_type=F32) + bias_ref[table, pair]

    s_next = scores(0)
    for job in range(n_tiles * n_pairs):
        t, pair = divmod(job, n_pairs)
        sl = slice(pair * LANES, (pair + 1) * LANES)
        rows = slice(t * nq, (t + 1) * nq)
        if pair == 0:
            m_tile = jnp.zeros((nq, LANES), F32)
            l_tile = jnp.ones((nq, LANES), F32)
        s = s_next
        if job + 1 < n_tiles * n_pairs:
            s_next = scores(job + 1)
        vv = keys_values(vp_ref, vc_ref, t, sl)
        zv = jnp.zeros_like(vv)
        m = jnp.max(s, axis=-1, keepdims=True)
        p = jnp.exp2(s - m).astype(BF16)
        pk = jnp.concatenate([p[:nq], p[nq:]], axis=1)
        rhs = jnp.concatenate(
            [jnp.concatenate([jnp.where(lo_k, vv, zv), ones_lo], axis=1),
             jnp.concatenate([jnp.where(lo_k, zv, vv), ones_hi], axis=1)], axis=0)
        ov = jnp.dot(pk, rhs, preferred_element_type=F32)
        o_ref[0, 0, rows, sl] = ov[:, :LANES].astype(BF16)
        here = stat_lane == pair
        m_tile = jnp.where(here, jnp.where(lo_q, m[:nq], m[nq:]), m_tile)
        l_tile = jnp.where(here, ov[:, LANES:], l_tile)
        if pair == n_pairs - 1:
            m_ref[0, 0, rows, :] = m_tile
            l_ref[0, 0, rows, :] = l_tile


def _attn_group(q, k, v, rel_bias, group, *, tq=512):
    bsz, dil, n_sub, width = q.shape
    tq = min(tq, n_sub)
    per_block = tq // B_BLOCK
    cur = lambda w: pl.BlockSpec((1, 1, tq, w), lambda b, r, i: (b, r, i, 0))
    prev = lambda w: pl.BlockSpec((1, 1, B_BLOCK, w),
                                  lambda b, r, i: (b, r, jnp.maximum(i * per_block - 1, 0), 0))
    stats = jax.ShapeDtypeStruct((bsz, dil, n_sub, LANES), F32)
    return pl.pallas_call(
        _attn_kernel,
        grid=(bsz, dil, n_sub // tq),
        in_specs=[cur(width), prev(width), cur(width), prev(width), cur(width),
                  _const_spec((2, B_HEADS // 2, 2 * B_BLOCK, 2 * B_BLOCK))],
        out_specs=[cur(width), cur(LANES), cur(LANES)],
        out_shape=[jax.ShapeDtypeStruct((bsz, dil, n_sub, width), BF16), stats, stats],
        compiler_params=pltpu.CompilerParams(
            dimension_semantics=("arbitrary", "arbitrary", "arbitrary"),
            vmem_limit_bytes=VMEM_LIMIT),
        name=f"dilated_attention_g{group}",
    )(q, k, k, v, v, _band_bias(rel_bias, group))


def _merge_kernel(x_ref, o0_ref, o1_ref, o2_ref, m0_ref, m1_ref, m2_ref, l0_ref, l1_ref, l2_ref,
                  ex_ref, wo_ref, out_ref, slab_ref, lslab_ref):
    tm = x_ref.shape[1]
    n_slabs = slab_ref.shape[0]

    def stat_rows(l_ref):
        dil = l_ref.shape[1]
        if dil == 1:
            return l_ref[0, 0]
        for r in range(dil):
            lslab_ref[pl.ds(r, tm // dil, stride=dil), :] = l_ref[0, r]
        return lslab_ref[...]

    def out_rows(o_ref):
        dil = o_ref.shape[1]
        if dil == 1:
            return o_ref[0, 0].astype(F32)
        for s in range(n_slabs):
            for r in range(dil):
                slab_ref[s, pl.ds(r, tm // dil, stride=dil), :] = (
                    o_ref[0, r, :, s * LANES:(s + 1) * LANES].astype(F32))
        return jnp.concatenate([slab_ref[s] for s in range(n_slabs)], axis=1)

    ms = [stat_rows(m_ref) for m_ref in (m0_ref, m1_ref, m2_ref)]
    ls = [stat_rows(l_ref) for l_ref in (l0_ref, l1_ref, l2_ref)]
    top = jnp.maximum(jnp.maximum(ms[0], ms[1]), ms[2])
    es = [jnp.exp2(m - top) for m in ms]
    inv = 1.0 / (es[0] * ls[0] + es[1] * ls[1] + es[2] * ls[2])
    acc = None
    for e, o_ref in zip(es, (o0_ref, o1_ref, o2_ref)):
        w = jnp.dot((e * inv).astype(BF16), ex_ref[...], preferred_element_type=F32)
        term = w * out_rows(o_ref)
        acc = term if acc is None else acc + term
    out_ref[0] = x_ref[0] + jnp.dot(acc.astype(BF16), wo_ref[...], preferred_element_type=F32)


def _merge_out(x, os, ms, ls, w_o, *, tm=512):
    bsz, seq, d = x.shape
    width = os[0].shape[-1]
    expand = np.zeros((LANES, width), np.float32)
    for h in range(B_HEADS):
        expand[_stat_lane(h), h * B_HEAD_DIM:(h + 1) * B_HEAD_DIM] = 1.0
    row = lambda w: pl.BlockSpec((1, tm, w), lambda b, i: (b, i, 0))
    grouped = lambda a: pl.BlockSpec((1, a.shape[1], tm // a.shape[1], a.shape[3]),
                                     lambda b, i: (b, 0, i, 0))
    return pl.pallas_call(
        _merge_kernel,
        grid=(bsz, seq // tm),
        in_specs=[row(d)] + [grouped(a) for a in (*os, *ms, *ls)]
                 + [_const_spec((LANES, width)), _const_spec((width, d))],
        out_specs=row(d),
        out_shape=jax.ShapeDtypeStruct((bsz, seq, d), F32),
        scratch_shapes=[pltpu.VMEM((width // LANES, tm, LANES), F32),
                        pltpu.VMEM((tm, LANES), F32)],
        compiler_params=pltpu.CompilerParams(
            dimension_semantics=("arbitrary", "arbitrary"), vmem_limit_bytes=VMEM_LIMIT),
        name="merge_out_projection",
    )(x, *os, *ms, *ls, jnp.asarray(expand, BF16), w_o.astype(BF16))


def kernel(x, norm_gain, ffn_w_in, ffn_w_out, a_w_in, a_lb_logits, a_out_gain, a_w_out,
           kv_norm, w_kv, k_gain, b_w_q, b_q_gain, b_w_o, rel_bias):
    bsz, seq, d = x.shape
    depth = norm_gain.shape[0]
    n_a = a_w_in.shape[0]
    t = bsz * seq
    ks = vs = None
    for layer in range(depth):
        if layer == n_a:
            outs = _project(x, kv_norm, w_kv, k_gain, n_kinds=2, n_normed=1, scale=1.0)
            ks, vs = outs[:B_GROUPS], outs[B_GROUPS:]
        x = _ffn(x.reshape(t, d), norm_gain[layer, 0], ffn_w_in[layer, 0],
                 ffn_w_out[layer, 0]).reshape(bsz, seq, d)
        if layer < n_a:
            x = _hgrn2_layer(x, norm_gain[layer, 1], a_w_in[layer], a_lb_logits, layer,
                             a_out_gain[layer], a_w_out[layer])
        else:
            j = layer - n_a
            qs = _project(x, norm_gain[layer, 1], b_w_q[j], b_q_gain[j],
                          n_kinds=1, n_normed=1, scale=B_HEAD_DIM ** -0.5 * LOG2E)
            parts = [_attn_group(qs[g], ks[g], vs[g], rel_bias, g) for g in range(B_GROUPS)]
            os, ms, ls = zip(*parts)
            x = _merge_out(x, os, ms, ls, b_w_o[j])
        x = _ffn(x.reshape(t, d), norm_gain[layer, 2], ffn_w_in[layer, 1],
                 ffn_w_out[layer, 1]).reshape(bsz, seq, d)
    return x
```

```python
import functools
import math

import jax
import jax.numpy as jnp
import numpy as np
from jax import lax
from jax.experimental import pallas as pl
from jax.experimental.pallas import tpu as pltpu

F32 = jnp.float32
BF16 = jnp.bfloat16

EPS = 1e-6
LOG2E = math.log2(math.e)
F32_TINY = float(np.finfo(np.float32).tiny)
LANES = 128
SUBLANES = 8
MXU_COLS = 256
VMEM_LIMIT = 56 * 1024 * 1024
VMEM_LIMIT_FUSED = 60 * 1024 * 1024

A_HEADS = 8
A_DIM = 128
A_CHUNK = 128
A_DIAG = SUBLANES
A_LEVELS = (16, 32, 64, 128)

B_WINDOWS = (128, 512, 2048)
B_DILATIONS = (1, 4, 16)
B_GROUPS = 3
B_HEADS = 16
B_HEAD_DIM = 64
B_BLOCK = 128
NUM_BUCKETS = 32
MAX_DISTANCE = 2048
NEG = -1e30

NT_DIMS = (((1,), (1,)), ((), ()))


def _const_spec(shape):
    nd = len(shape)
    return pl.BlockSpec(shape, lambda *_: (0,) * nd, pipeline_mode=pl.Buffered(1))


def _sigmoid(x):
    return 1.0 / (1.0 + jnp.exp(-x))


def _rms_rows(x, gain):
    ms = jnp.mean(x * x, axis=-1, keepdims=True)
    return x * lax.rsqrt(ms + EPS) * gain


def _half_lane_mask(shape):
    return lax.broadcasted_iota(jnp.int32, shape, len(shape) - 1) < (LANES // 2)


def _head_rms_pair(t, gain_row):
    lo = _half_lane_mask(t.shape)
    sq = t * t
    s_lo = jnp.sum(jnp.where(lo, sq, 0.0), axis=-1, keepdims=True)
    s_hi = jnp.sum(jnp.where(lo, 0.0, sq), axis=-1, keepdims=True)
    ms = jnp.where(lo, s_lo, s_hi) * (1.0 / B_HEAD_DIM)
    return t * lax.rsqrt(ms + EPS) * gain_row


def _ffn_jobs(x_ref, gain_ref, wg_ref, wu_ref, wo_ref, h_ref, act_ref, dst_ref):
    d_ff = wg_ref.shape[1]
    d = wo_ref.shape[1]

    def norm():
        h_ref[...] = _rms_rows(x_ref[...], gain_ref[...]).astype(BF16)

    def hidden(c0):
        def run():
            sl = slice(c0, c0 + MXU_COLS)
            h = h_ref[...]
            g = jnp.dot(h, wg_ref[:, sl], preferred_element_type=F32)
            u = jnp.dot(h, wu_ref[:, sl], preferred_element_type=F32)
            act_ref[:, sl] = (g * _sigmoid(g) * u).astype(BF16)
        return run

    def output(c0):
        def run():
            sl = slice(c0, c0 + MXU_COLS)
            y = jnp.dot(act_ref[...], wo_ref[:, sl], preferred_element_type=F32)
            dst_ref[:, sl] = x_ref[:, sl] + 0.5 * y
        return run

    return ([norm] + [hidden(c0) for c0 in range(0, d_ff, MXU_COLS)]
            + [output(c0) for c0 in range(0, d, MXU_COLS)])


def _ffn_kernel(x_ref, gain_ref, wg_ref, wu_ref, wo_ref, o_ref, h_ref, act_ref):
    for job in _ffn_jobs(x_ref, gain_ref, wg_ref, wu_ref, wo_ref, h_ref, act_ref, o_ref):
        job()


def _ffn_weights(w_in, w_out):
    d_ff = w_out.shape[0]
    return w_in[:, :d_ff].astype(BF16), w_in[:, d_ff:].astype(BF16), w_out.astype(BF16)


def _ffn(x2, gain, w_in, w_out, *, tm=1024):
    t, d = x2.shape
    d_ff = w_out.shape[0]
    return pl.pallas_call(
        _ffn_kernel,
        grid=(t // tm,),
        in_specs=[
            pl.BlockSpec((tm, d), lambda i: (i, 0)),
            _const_spec((1, d)),
            _const_spec((d, d_ff)),
            _const_spec((d, d_ff)),
            _const_spec((d_ff, d)),
        ],
        out_specs=pl.BlockSpec((tm, d), lambda i: (i, 0)),
        out_shape=jax.ShapeDtypeStruct((t, d), F32),
        scratch_shapes=[pltpu.VMEM((tm, d), BF16), pltpu.VMEM((tm, d_ff), BF16)],
        compiler_params=pltpu.CompilerParams(
            dimension_semantics=("arbitrary",), vmem_limit_bytes=VMEM_LIMIT),
        name="swiglu_half_step",
    )(x2, gain.reshape(1, d), *_ffn_weights(w_in, w_out))


def _hgrn2_gates(p_ref, e, lb, q_ref, k_ref, c_ref, v_ref, g_ref):
    qp, fp, v, gp = (p_ref[:, (2 * i + e) * A_DIM:(2 * i + e + 1) * A_DIM] for i in range(4))
    z = jnp.exp(-jnp.abs(fp))
    r = 1.0 / (1.0 + z)
    pos = fp >= 0
    sig = jnp.where(pos, r, z * r)
    one_minus_sig = jnp.where(pos, z * r, r)
    f = jnp.maximum(lb + (1.0 - lb) * sig, F32_TINY)
    k = (1.0 - lb) * one_minus_sig
    q_ref[e] = qp * _sigmoid(qp)
    k_ref[e] = k
    c_ref[e] = jnp.log(k) * LOG2E
    v_ref[e] = v
    g_ref[e] = gp * _sigmoid(gp)
    return jnp.log(f) * LOG2E


def _hgrn2_diag(q, b2, cs, tri):
    nb = A_CHUNK // A_DIAG
    q3, b3, c3 = (t.reshape(nb, A_DIAG, LANES) for t in (q, b2, cs))
    lane3 = lax.broadcasted_iota(jnp.int32, (nb, A_DIAG, LANES), 2)
    ch = jnp.zeros((nb, A_DIAG, LANES), F32)
    for s in range(A_DIAG):
        e = jnp.exp2(b3 - c3[:, s:s + 1])
        col = jnp.sum(q3 * e, axis=-1, keepdims=True)
        ch = jnp.where(lane3 == s, col, ch)
    return jnp.where(tri >= 0, ch.reshape(A_CHUNK, LANES), 0.0).astype(BF16)


def _hgrn2_levels(a, q, k, b2, xr):
    c = A_CHUNK
    for level in A_LEVELS:
        half = level // 2
        shape4 = (c // level, 2, half, LANES)
        q4, k4, b4 = (t.reshape(shape4) for t in (q, k, b2))
        b_mid = b4[:, 0, half - 1:half]
        zero = jnp.zeros((c // level, half, LANES), F32)
        qu = q4[:, 1] * jnp.exp2(b4[:, 1] - b_mid)
        kl = k4[:, 0] * jnp.exp2(b_mid - b4[:, 0])
        ql = jnp.concatenate([zero, qu], axis=1).reshape(c, LANES).astype(BF16)
        kl = jnp.concatenate([kl, zero], axis=1).reshape(c, LANES).astype(BF16)
        al = lax.dot_general(ql, kl, NT_DIMS, preferred_element_type=F32)
        a = jnp.where(xr < half, a, al)
    return a


def _hgrn2_pair(p_ref, j, fillers, lbl_ref, og_ref, wo_ref, tril, tile, xr, tri, o_ref,
                q_ref, k_ref, b_ref, c_ref, v_ref, g_ref, cm_ref, a_ref, st_refs, oh_ref,
                *, lb_slots):
    gate_fillers, diag_fillers, level_fillers, rec_fillers = (list(f) for f in fillers)
    tt = p_ref.shape[0]
    n_ch = tt // A_CHUNK
    jobs = [(e, c) for e in range(2) for c in range(n_ch)]
    rows = lambda c: slice(c * A_CHUNK, (c + 1) * A_CHUNK)

    hi_parts, lo_parts = [], []
    for e in range(2):
        lg = lbl_ref[2 * j + e]
        ex = jnp.exp(lg - jnp.max(lg, axis=0, keepdims=True))
        lb = (jnp.sum(ex[:lb_slots], axis=0, keepdims=True)
              / jnp.sum(ex, axis=0, keepdims=True))
        lf2 = _hgrn2_gates(p_ref, e, lb, q_ref, k_ref, c_ref, v_ref, g_ref)
        if gate_fillers:
            gate_fillers.pop(0)()
        hi = lf2.astype(BF16)
        lo = (lf2 - hi.astype(F32)).astype(BF16)
        hi_parts += [hi[rows(c)] for c in range(n_ch)]
        lo_parts += [lo[rows(c)] for c in range(n_ch)]
    b_all = (jnp.dot(tril, jnp.concatenate(hi_parts, axis=1), preferred_element_type=F32)
             + jnp.dot(tril, jnp.concatenate(lo_parts, axis=1), preferred_element_type=F32))
    for i, (e, c) in enumerate(jobs):
        b2 = b_all[:, i * LANES:(i + 1) * LANES]
        b_ref[e, rows(c), :] = b2
        c_ref[e, rows(c), :] = b2 - c_ref[e, rows(c), :]

    def run_fillers(pending, every):
        return pending.pop(0)() if pending and every else None

    for i, (e, c) in enumerate(jobs):
        cm_ref[i * A_CHUNK:(i + 1) * A_CHUNK, :] = _hgrn2_diag(
            q_ref[e, rows(c), :], b_ref[e, rows(c), :], c_ref[e, rows(c), :], tri)
        run_fillers(diag_fillers, True)
    for job in diag_fillers:
        job()
    ad_all = jnp.dot(cm_ref[...], tile, preferred_element_type=F32)

    for i, (e, c) in enumerate(jobs):
        a = _hgrn2_levels(ad_all[i * A_CHUNK:(i + 1) * A_CHUNK], q_ref[e, rows(c), :],
                          k_ref[e, rows(c), :], b_ref[e, rows(c), :], xr)
        a_ref[i] = a.astype(BF16)
        run_fillers(level_fillers, i % 2 == 1)
    for job in level_fillers:
        job()

    for c in range(n_ch):
        for e in range(2):
            st_ref = st_refs[e]
            q, k, b2, v = (r[e, rows(c), :] for r in (q_ref, k_ref, b_ref, v_ref))
            st = st_ref[j]
            b_last = b2[A_CHUNK - 1:A_CHUNK, :]
            q_in = (q * jnp.exp2(b2)).astype(BF16)
            o = (jnp.dot(a_ref[e * n_ch + c], v.astype(BF16), preferred_element_type=F32)
                 + lax.dot_general(q_in, st.astype(BF16), NT_DIMS, preferred_element_type=F32))
            k_out = (k * jnp.exp2(b_last - b2)).astype(BF16)
            st_ref[j] = st * jnp.exp2(b_last) + jnp.dot(v.T.astype(BF16), k_out,
                                                         preferred_element_type=F32)
            o = _rms_rows(o, og_ref[...]) * g_ref[e, rows(c), :]
            oh_ref[rows(c), e * A_DIM:(e + 1) * A_DIM] = o.astype(BF16)
        run_fillers(rec_fillers, True)
    for job in rec_fillers:
        job()
    o_ref[...] += jnp.dot(oh_ref[...], wo_ref[j], preferred_element_type=F32)


def _ffn_hgrn2_kernel(x_ref, fgain_ref, wg_ref, wu_ref, wfo_ref, gain_ref, w_ref, lbl_ref, og_ref,
                      wo_ref, tril_ref, tile_ref, o_ref, u_ref, un_ref, fh_ref, act_ref,
                      h_ref, pa_ref, pb_ref, q_ref, k_ref, b_ref, c_ref, v_ref, g_ref, cm_ref,
                      a_ref, st0_ref, st1_ref, oh_ref, *, lb_slots, tiles_per_seq):
    step = pl.program_id(0)

    @pl.when(step == 0)
    def _():
        u_ref[...] = jnp.zeros_like(u_ref)

    @pl.when(lax.rem(jnp.maximum(step - 1, 0), tiles_per_seq) == 0)
    def _():
        st0_ref[...] = jnp.zeros_like(st0_ref)
        st1_ref[...] = jnp.zeros_like(st1_ref)

    ffn_jobs = _ffn_jobs(x_ref, fgain_ref, wg_ref, wu_ref, wfo_ref, fh_ref, act_ref, un_ref)
    x = u_ref[...]
    o_ref[...] = x
    h_ref[...] = _rms_rows(x, gain_ref[...]).astype(BF16)
    tril = tril_ref[...]
    tile = tile_ref[...]
    row = lax.broadcasted_iota(jnp.int32, (A_CHUNK, LANES), 0)
    lane = lax.broadcasted_iota(jnp.int32, (A_CHUNK, LANES), 1)
    xr = jnp.bitwise_xor(row, lane)
    tri = jnp.bitwise_and(row, A_DIAG - 1) - lane
    n_pairs = wo_ref.shape[0]
    part_width = w_ref.shape[1] // 4

    def projection_jobs(dst_ref, pair):
        def job(part):
            def run():
                src = part * part_width + pair * 2 * A_DIM
                dst_ref[:, part * 2 * A_DIM:(part + 1) * 2 * A_DIM] = jnp.dot(
                    h_ref[...], w_ref[:, src:src + 2 * A_DIM], preferred_element_type=F32)
            return run
        return [job(part) for part in range(4)]

    common = dict(lbl_ref=lbl_ref, og_ref=og_ref, wo_ref=wo_ref, tril=tril, tile=tile, xr=xr,
                  tri=tri, o_ref=o_ref, q_ref=q_ref, k_ref=k_ref, b_ref=b_ref, c_ref=c_ref,
                  v_ref=v_ref, g_ref=g_ref, cm_ref=cm_ref, a_ref=a_ref,
                  st_refs=(st0_ref, st1_ref), oh_ref=oh_ref, lb_slots=lb_slots)

    for job in projection_jobs(pa_ref, 0):
        job()

    bufs = (pa_ref, pb_ref)
    per_pair = -(-len(ffn_jobs) // n_pairs)
    for j in range(n_pairs):
        ahead = projection_jobs(bufs[(j + 1) % 2], j + 1) if j + 1 < n_pairs else []
        mine = ffn_jobs[j * per_pair:(j + 1) * per_pair]
        diag_fillers = mine[:1] + ahead[:2] + mine[1:2] + ahead[2:]
        _hgrn2_pair(bufs[j % 2], j, ([], diag_fillers, mine[2:3], mine[3:]), **common)
    u_ref[...] = un_ref[...]


def _ffn_hgrn2_layer(x2, seq, fgain, f_w_in, f_w_out, gain, w_in, lb_logits, layer, out_gain,
                     w_out, *, tt=512):
    t, d = x2.shape
    d_ff = f_w_out.shape[0]
    n_tiles = t // tt
    n_slots = lb_logits.shape[0]
    n_pairs = A_HEADS // 2
    n_jobs = 2 * (tt // A_CHUNK)
    wo_pairs = w_out.reshape(n_pairs, 2 * A_DIM, d).astype(BF16)
    lbl = lb_logits.reshape(n_slots, A_HEADS, A_DIM).transpose(1, 0, 2)
    tril = jnp.asarray(np.tril(np.ones((A_CHUNK, A_CHUNK), np.float32)), BF16)
    idx = np.arange(LANES)
    tile = jnp.asarray((idx[:, None] == idx[None, :] % A_DIAG).astype(np.float32), BF16)
    head_tile = lambda dtype: pltpu.VMEM((2, tt, A_DIM), dtype)
    return pl.pallas_call(
        functools.partial(_ffn_hgrn2_kernel, lb_slots=layer + 1, tiles_per_seq=seq // tt),
        grid=(n_tiles + 1,),
        in_specs=[
            pl.BlockSpec((tt, d), lambda s: (jnp.minimum(s, n_tiles - 1), 0)),
            _const_spec((1, d)),
            _const_spec((d, d_ff)),
            _const_spec((d, d_ff)),
            _const_spec((d_ff, d)),
            _const_spec((1, d)),
            _const_spec(w_in.shape),
            _const_spec((A_HEADS, n_slots, A_DIM)),
            _const_spec((1, A_DIM)),
            _const_spec((n_pairs, 2 * A_DIM, d)),
            _const_spec((A_CHUNK, A_CHUNK)),
            _const_spec((LANES, LANES)),
        ],
        out_specs=pl.BlockSpec((tt, d), lambda s: (jnp.maximum(s - 1, 0), 0)),
        out_shape=jax.ShapeDtypeStruct((t, d), F32),
        scratch_shapes=[
            pltpu.VMEM((tt, d), F32),
            pltpu.VMEM((tt, d), F32),
            pltpu.VMEM((tt, d), BF16),
            pltpu.VMEM((tt, d_ff), BF16),
            pltpu.VMEM((tt, d), BF16),
            pltpu.VMEM((tt, 8 * A_DIM), F32),
            pltpu.VMEM((tt, 8 * A_DIM), F32),
            head_tile(F32), head_tile(F32), head_tile(F32),
            head_tile(F32), head_tile(F32), head_tile(F32),
            pltpu.VMEM((n_jobs * A_CHUNK, LANES), BF16),
            pltpu.VMEM((n_jobs, A_CHUNK, LANES), BF16),
            pltpu.VMEM((n_pairs, A_DIM, A_DIM), F32),
            pltpu.VMEM((n_pairs, A_DIM, A_DIM), F32),
            pltpu.VMEM((tt, 2 * A_DIM), BF16),
        ],
        compiler_params=pltpu.CompilerParams(
            dimension_semantics=("arbitrary",), vmem_limit_bytes=VMEM_LIMIT_FUSED),
        name="swiglu_hgrn2_layer",
    )(x2, fgain.reshape(1, d), *_ffn_weights(f_w_in, f_w_out), gain.reshape(1, d),
      w_in.astype(BF16), lbl, out_gain.reshape(1, A_DIM), wo_pairs, tril, tile)


def _proj_kernel(x_ref, gain_ref, w_ref, hg_ref, *refs, n_normed, scale):
    out_refs, hs_ref = refs[:-1], refs[-1]
    tm = x_ref.shape[1]
    width = out_refs[0].shape[-1]
    hf = _rms_rows(x_ref[0], gain_ref[...])
    n_slabs = hs_ref.shape[0]
    for s in range(n_slabs):
        hs_ref[s] = hf[:, s * LANES:(s + 1) * LANES]
    lhs = {}
    for dil in sorted(set(B_DILATIONS)):
        if dil == 1:
            lhs[dil] = hf.astype(BF16)
            continue
        slabs = [jnp.concatenate([hs_ref[s, pl.ds(r, tm // dil, stride=dil), :]
                                  for r in range(dil)], axis=0).astype(BF16)
                 for s in range(n_slabs)]
        lhs[dil] = jnp.concatenate(slabs, axis=1)
    for j, o_ref in enumerate(out_refs):
        kind, g = divmod(j, B_GROUPS)
        dil = B_DILATIONS[g]
        for c0 in range(0, width, MXU_COLS):
            col = j * width + c0
            t2 = jnp.dot(lhs[dil], w_ref[:, col:col + MXU_COLS], preferred_element_type=F32)
            for c1 in range(c0, c0 + MXU_COLS, LANES):
                t = t2[:, c1 - c0:c1 - c0 + LANES]
                if kind < n_normed:
                    t = _head_rms_pair(t, hg_ref[g:g + 1, c1:c1 + LANES]) * scale
                o_ref[0, :, :, c1:c1 + LANES] = t.astype(BF16).reshape(dil, tm // dil, LANES)


def _project(x, gain, w, head_gain, *, n_kinds, n_normed, scale, tm=512):
    bsz, seq, d = x.shape
    n_out = n_kinds * B_GROUPS
    width = w.shape[1] // n_out
    hg = jnp.tile(head_gain, (1, B_HEADS))
    dils = [B_DILATIONS[j % B_GROUPS] for j in range(n_out)]
    outs = pl.pallas_call(
        functools.partial(_proj_kernel, n_normed=n_normed, scale=scale),
        grid=(bsz, seq // tm),
        in_specs=[
            pl.BlockSpec((1, tm, d), lambda b, i: (b, i, 0)),
            _const_spec((1, d)),
            _const_spec((d, n_out * width)),
            _const_spec((B_GROUPS, width)),
        ],
        out_specs=[pl.BlockSpec((1, dil, tm // dil, width), lambda b, i: (b, 0, i, 0))
                   for dil in dils],
        out_shape=[jax.ShapeDtypeStruct((bsz, dil, seq // dil, width), BF16) for dil in dils],
        scratch_shapes=[pltpu.VMEM((d // LANES, tm, LANES), F32)],
        compiler_params=pltpu.CompilerParams(
            dimension_semantics=("arbitrary", "arbitrary"), vmem_limit_bytes=VMEM_LIMIT),
        name="norm_projection",
    )(x, gain.reshape(1, d), w.astype(BF16), hg)
    return outs


def _t5_bucket(dist):
    dist = np.asarray(dist, np.int32)
    max_exact = NUM_BUCKETS // 2
    large = max_exact + (np.log(np.maximum(dist, 1) / max_exact)
                         / math.log(MAX_DISTANCE / max_exact) * (NUM_BUCKETS - max_exact)).astype(np.int32)
    large = np.minimum(large, NUM_BUCKETS - 1)
    return np.where(dist < max_exact, dist, large).astype(np.int32)


def _band_bias(rel_bias, group):
    dil = B_DILATIONS[group]
    wd = B_WINDOWS[group] // dil
    i = np.arange(B_BLOCK)[:, None]
    j = np.arange(2 * B_BLOCK)[None, :]
    dist = i + wd - j
    band = (dist >= 0) & (dist <= wd)
    bucket = _t5_bucket(np.clip(dist, 0, wd) * dil)
    ids = jnp.asarray(np.where(band, bucket, -1), jnp.int32)
    onehot = (ids[None] == jnp.arange(NUM_BUCKETS, dtype=jnp.int32)[:, None, None]).astype(F32)
    table = rel_bias[:, group * B_HEADS:(group + 1) * B_HEADS].astype(F32)
    bias = jnp.einsum("bh,bij->hij", table, onehot,
                      precision=lax.Precision.HIGHEST) * LOG2E
    pair_shape = (B_HEADS // 2, 2 * B_BLOCK, 2 * B_BLOCK)
    tables = []
    for keep in (band & (j >= B_BLOCK), band):
        tables.append((bias + jnp.asarray(np.where(keep, 0.0, NEG), F32)[None]).reshape(pair_shape))
    return jnp.stack(tables)


def _stat_lane(head):
    return (head % 2) * (LANES // 2) + head // 2


def _attn_kernel(q_ref, kp_ref, kc_ref, vp_ref, vc_ref, bias_ref, o_ref, m_ref, l_ref):
    nq = B_BLOCK
    half = LANES // 2
    n_tiles = q_ref.shape[2] // nq
    n_pairs = B_HEADS // 2
    first_table = jnp.where(pl.program_id(2) == 0, 0, 1)
    lo_q = _half_lane_mask((nq, LANES))
    lo_k = _half_lane_mask((2 * nq, LANES))
    lane = lax.broadcasted_iota(jnp.int32, (nq, LANES), 1)
    stat_lane = jnp.bitwise_and(lane, half - 1)
    ones_lo = jnp.where(lo_k, 1.0, 0.0).astype(BF16)
    ones_hi = jnp.where(lo_k, 0.0, 1.0).astype(BF16)

    def keys_values(prev_ref, cur_ref, t, sl):
        if t == 0:
            return jnp.concatenate([prev_ref[0, 0, :, sl], cur_ref[0, 0, :nq, sl]], axis=0)
        return cur_ref[0, 0, (t - 1) * nq:(t + 1) * nq, sl]

    def scores(job):
        t, pair = divmod(job, n_pairs)
        sl = slice(pair * LANES, (pair + 1) * LANES)
        qp = q_ref[0, 0, t * nq:(t + 1) * nq, sl]
        kk = keys_values(kp_ref, kc_ref, t, sl)
        zq = jnp.zeros_like(qp)
        qq = jnp.concatenate([jnp.where(lo_q, qp, zq), jnp.where(lo_q, zq, qp)], axis=0)
        table = first_table if t == 0 else 1
        return lax.dot_general(qq, kk, NT_DIMS, preferred_element_type=F32) + bias_ref[table, pair]

    s_next = scores(0)
    for job in range(n_tiles * n_pairs):
        t, pair = divmod(job, n_pairs)
        sl = slice(pair * LANES, (pair + 1) * LANES)
        rows = slice(t * nq, (t + 1) * nq)
        if pair == 0:
            m_tile = jnp.zeros((nq, LANES), F32)
            l_tile = jnp.ones((nq, LANES), F32)
        s = s_next
        if job + 1 < n_tiles * n_pairs:
            s_next = scores(job + 1)
        vv = keys_values(vp_ref, vc_ref, t, sl)
        zv = jnp.zeros_like(vv)
        m = jnp.max(s, axis=-1, keepdims=True)
        p = jnp.exp2(s - m).astype(BF16)
        pk = jnp.concatenate([p[:nq], p[nq:]], axis=1)
        rhs = jnp.concatenate(
            [jnp.concatenate([jnp.where(lo_k, vv, zv), ones_lo], axis=1),
             jnp.concatenate([jnp.where(lo_k, zv, vv), ones_hi], axis=1)], axis=0)
        ov = jnp.dot(pk, rhs, preferred_element_type=F32)
        o_ref[0, 0, rows, sl] = ov[:, :LANES].astype(BF16)
        here = stat_lane == pair
        m_tile = jnp.where(here, jnp.where(lo_q, m[:nq], m[nq:]), m_tile)
        l_tile = jnp.where(here, ov[:, LANES:], l_tile)
        if pair == n_pairs - 1:
            m_ref[0, 0, rows, :] = m_tile
            l_ref[0, 0, rows, :] = l_tile


def _attn_group(q, k, v, rel_bias, group, *, tq=1024):
    bsz, dil, n_sub, width = q.shape
    tq = min(tq, n_sub)
    per_block = tq // B_BLOCK
    cur = lambda w: pl.BlockSpec((1, 1, tq, w), lambda b, r, i: (b, r, i, 0))
    prev = lambda w: pl.BlockSpec((1, 1, B_BLOCK, w),
                                  lambda b, r, i: (b, r, jnp.maximum(i * per_block - 1, 0), 0))
    stats = jax.ShapeDtypeStruct((bsz, dil, n_sub, LANES), F32)
    return pl.pallas_call(
        _attn_kernel,
        grid=(bsz, dil, n_sub // tq),
        in_specs=[cur(width), prev(width), cur(width), prev(width), cur(width),
                  _const_spec((2, B_HEADS // 2, 2 * B_BLOCK, 2 * B_BLOCK))],
        out_specs=[cur(width), cur(LANES), cur(LANES)],
        out_shape=[jax.ShapeDtypeStruct((bsz, dil, n_sub, width), BF16), stats, stats],
        compiler_params=pltpu.CompilerParams(
            dimension_semantics=("arbitrary", "arbitrary", "arbitrary"),
            vmem_limit_bytes=VMEM_LIMIT),
        name=f"dilated_attention_g{group}",
    )(q, k, k, v, v, _band_bias(rel_bias, group))


def _merge_kernel(x_ref, o0_ref, o1_ref, o2_ref, m0_ref, m1_ref, m2_ref, l0_ref, l1_ref, l2_ref,
                  ex_ref, wo_ref, out_ref, slab_ref, lslab_ref):
    tm = x_ref.shape[1]
    n_slabs = slab_ref.shape[0]

    def stat_rows(l_ref):
        dil = l_ref.shape[1]
        if dil == 1:
            return l_ref[0, 0]
        for r in range(dil):
            lslab_ref[pl.ds(r, tm // dil, stride=dil), :] = l_ref[0, r]
        return lslab_ref[...]

    def out_rows(o_ref):
        dil = o_ref.shape[1]
        if dil == 1:
            return o_ref[0, 0].astype(F32)
        for s in range(n_slabs):
            for r in range(dil):
                slab_ref[s, pl.ds(r, tm // dil, stride=dil), :] = (
                    o_ref[0, r, :, s * LANES:(s + 1) * LANES].astype(F32))
        return jnp.concatenate([slab_ref[s] for s in range(n_slabs)], axis=1)

    ms = [stat_rows(m_ref) for m_ref in (m0_ref, m1_ref, m2_ref)]
    ls = [stat_rows(l_ref) for l_ref in (l0_ref, l1_ref, l2_ref)]
    top = jnp.maximum(jnp.maximum(ms[0], ms[1]), ms[2])
    es = [jnp.exp2(m - top) for m in ms]
    inv = 1.0 / (es[0] * ls[0] + es[1] * ls[1] + es[2] * ls[2])
    acc = None
    for e, o_ref in zip(es, (o0_ref, o1_ref, o2_ref)):
        w = jnp.dot((e * inv).astype(BF16), ex_ref[...], preferred_element_type=F32)
        term = w * out_rows(o_ref)
        acc = term if acc is None else acc + term
    out_ref[0] = x_ref[0] + jnp.dot(acc.astype(BF16), wo_ref[...], preferred_element_type=F32)


def _merge_out(x, os, ms, ls, w_o, *, tm=512):
    bsz, seq, d = x.shape
    width = os[0].shape[-1]
    expand = np.zeros((LANES, width), np.float32)
    for h in range(B_HEADS):
        expand[_stat_lane(h), h * B_HEAD_DIM:(h + 1) * B_HEAD_DIM] = 1.0
    row = lambda w: pl.BlockSpec((1, tm, w), lambda b, i: (b, i, 0))
    grouped = lambda a: pl.BlockSpec((1, a.shape[1], tm // a.shape[1], a.shape[3]),
                                     lambda b, i: (b, 0, i, 0))
    return pl.pallas_call(
        _merge_kernel,
        grid=(bsz, seq // tm),
        in_specs=[row(d)] + [grouped(a) for a in (*os, *ms, *ls)]
                 + [_const_spec((LANES, width)), _const_spec((width, d))],
        out_specs=row(d),
        out_shape=jax.ShapeDtypeStruct((bsz, seq, d), F32),
        scratch_shapes=[pltpu.VMEM((width // LANES, tm, LANES), F32),
                        pltpu.VMEM((tm, LANES), F32)],
        compiler_params=pltpu.CompilerParams(
            dimension_semantics=("arbitrary", "arbitrary"), vmem_limit_bytes=VMEM_LIMIT),
        name="merge_out_projection",
    )(x, *os, *ms, *ls, jnp.asarray(expand, BF16), w_o.astype(BF16))


def kernel(x, norm_gain, ffn_w_in, ffn_w_out, a_w_in, a_lb_logits, a_out_gain, a_w_out,
           kv_norm, w_kv, k_gain, b_w_q, b_q_gain, b_w_o, rel_bias):
    bsz, seq, d = x.shape
    depth = norm_gain.shape[0]
    n_a = a_w_in.shape[0]
    t = bsz * seq
    ks = vs = None
    for layer in range(depth):
        if layer == n_a:
            outs = _project(x, kv_norm, w_kv, k_gain, n_kinds=2, n_normed=1, scale=1.0)
            ks, vs = outs[:B_GROUPS], outs[B_GROUPS:]
        if layer < n_a:
            x = _ffn_hgrn2_layer(x.reshape(t, d), seq, norm_gain[layer, 0], ffn_w_in[layer, 0],
                                 ffn_w_out[layer, 0], norm_gain[layer, 1], a_w_in[layer],
                                 a_lb_logits, layer, a_out_gain[layer],
                                 a_w_out[layer]).reshape(bsz, seq, d)
        else:
            j = layer - n_a
            x = _ffn(x.reshape(t, d), norm_gain[layer, 0], ffn_w_in[layer, 0],
                     ffn_w_out[layer, 0]).reshape(bsz, seq, d)
            qs = _project(x, norm_gain[layer, 1], b_w_q[j], b_q_gain[j],
                          n_kinds=1, n_normed=1, scale=B_HEAD_DIM ** -0.5 * LOG2E)
            parts = [_attn_group(qs[g], ks[g], vs[g], rel_bias, g) for g in range(B_GROUPS)]
            os, ms, ls = zip(*parts)
            x = _merge_out(x, os, ms, ls, b_w_o[j])
        x = _ffn(x.reshape(t, d), norm_gain[layer, 2], ffn_w_in[layer, 1],
                 ffn_w_out[layer, 1]).reshape(bsz, seq, d)
    return x
```

```python
import functools
import math

import jax
import jax.numpy as jnp
import numpy as np
from jax import lax
from jax.experimental import pallas as pl
from jax.experimental.pallas import tpu as pltpu

F32 = jnp.float32
BF16 = jnp.bfloat16

EPS = 1e-6
LOG2E = math.log2(math.e)
F32_TINY = float(np.finfo(np.float32).tiny)
LANES = 128
SUBLANES = 8
MXU_COLS = 256
VMEM_LIMIT = 56 * 1024 * 1024
VMEM_LIMIT_FUSED = 60 * 1024 * 1024

A_HEADS = 8
A_DIM = 128
A_CHUNK = 128
A_DIAG = SUBLANES
A_LEVELS = (16, 32, 64, 128)

B_WINDOWS = (128, 512, 2048)
B_DILATIONS = (1, 4, 16)
B_GROUPS = 3
B_HEADS = 16
B_HEAD_DIM = 64
B_BLOCK = 128
NUM_BUCKETS = 32
MAX_DISTANCE = 2048
NEG = -1e30

NT_DIMS = (((1,), (1,)), ((), ()))


def _const_spec(shape):
    nd = len(shape)
    return pl.BlockSpec(shape, lambda *_: (0,) * nd, pipeline_mode=pl.Buffered(1))


def _sigmoid(x):
    return 1.0 / (1.0 + jnp.exp(-x))


def _rms_rows(x, gain):
    ms = jnp.mean(x * x, axis=-1, keepdims=True)
    return x * lax.rsqrt(ms + EPS) * gain


def _half_lane_mask(shape):
    return lax.broadcasted_iota(jnp.int32, shape, len(shape) - 1) < (LANES // 2)


def _head_rms_pair(t, gain_row):
    lo = _half_lane_mask(t.shape)
    sq = t * t
    s_lo = jnp.sum(jnp.where(lo, sq, 0.0), axis=-1, keepdims=True)
    s_hi = jnp.sum(jnp.where(lo, 0.0, sq), axis=-1, keepdims=True)
    ms = jnp.where(lo, s_lo, s_hi) * (1.0 / B_HEAD_DIM)
    return t * lax.rsqrt(ms + EPS) * gain_row


def _ffn_jobs(x_ref, gain_ref, wg_ref, wu_ref, wo_ref, h_ref, act_ref, dst_ref):
    d_ff = wg_ref.shape[1]
    d = wo_ref.shape[1]

    def norm():
        h_ref[...] = _rms_rows(x_ref[...], gain_ref[...]).astype(BF16)

    def hidden(c0):
        def run():
            sl = slice(c0, c0 + MXU_COLS)
            h = h_ref[...]
            g = jnp.dot(h, wg_ref[:, sl], preferred_element_type=F32)
            u = jnp.dot(h, wu_ref[:, sl], preferred_element_type=F32)
            act_ref[:, sl] = (g * _sigmoid(g) * u).astype(BF16)
        return run

    def output(c0):
        def run():
            sl = slice(c0, c0 + MXU_COLS)
            y = jnp.dot(act_ref[...], wo_ref[:, sl], preferred_element_type=F32)
            dst_ref[:, sl] = x_ref[:, sl] + 0.5 * y
        return run

    return ([norm] + [hidden(c0) for c0 in range(0, d_ff, MXU_COLS)]
            + [output(c0) for c0 in range(0, d, MXU_COLS)])


def _ffn_kernel(x_ref, gain_ref, wg_ref, wu_ref, wo_ref, o_ref, h_ref, act_ref):
    for job in _ffn_jobs(x_ref, gain_ref, wg_ref, wu_ref, wo_ref, h_ref, act_ref, o_ref):
        job()


def _ffn_weights(w_in, w_out):
    d_ff = w_out.shape[0]
    return w_in[:, :d_ff].astype(BF16), w_in[:, d_ff:].astype(BF16), w_out.astype(BF16)


def _ffn(x2, gain, w_in, w_out, *, tm=1024):
    t, d = x2.shape
    d_ff = w_out.shape[0]
    return pl.pallas_call(
        _ffn_kernel,
        grid=(t // tm,),
        in_specs=[
            pl.BlockSpec((tm, d), lambda i: (i, 0)),
            _const_spec((1, d)),
            _const_spec((d, d_ff)),
            _const_spec((d, d_ff)),
            _const_spec((d_ff, d)),
        ],
        out_specs=pl.BlockSpec((tm, d), lambda i: (i, 0)),
        out_shape=jax.ShapeDtypeStruct((t, d), F32),
        scratch_shapes=[pltpu.VMEM((tm, d), BF16), pltpu.VMEM((tm, d_ff), BF16)],
        compiler_params=pltpu.CompilerParams(
            dimension_semantics=("arbitrary",), vmem_limit_bytes=VMEM_LIMIT),
        name="swiglu_half_step",
    )(x2, gain.reshape(1, d), *_ffn_weights(w_in, w_out))


def _hgrn2_gates(p_ref, e, lb, q_ref, k_ref, c_ref, v_ref, g_ref):
    qp, fp, v, gp = (p_ref[:, (2 * i + e) * A_DIM:(2 * i + e + 1) * A_DIM] for i in range(4))
    z = jnp.exp(-jnp.abs(fp))
    r = 1.0 / (1.0 + z)
    pos = fp >= 0
    sig = jnp.where(pos, r, z * r)
    one_minus_sig = jnp.where(pos, z * r, r)
    f = jnp.maximum(lb + (1.0 - lb) * sig, F32_TINY)
    k = (1.0 - lb) * one_minus_sig
    q_ref[e] = qp * _sigmoid(qp)
    k_ref[e] = k
    c_ref[e] = jnp.log(k) * LOG2E
    v_ref[e] = v
    g_ref[e] = gp * _sigmoid(gp)
    return jnp.log(f) * LOG2E


def _hgrn2_diag(q, b2, cs, tri):
    nb = A_CHUNK // A_DIAG
    q3, b3, c3 = (t.reshape(nb, A_DIAG, LANES) for t in (q, b2, cs))
    lane3 = lax.broadcasted_iota(jnp.int32, (nb, A_DIAG, LANES), 2)
    ch = jnp.zeros((nb, A_DIAG, LANES), F32)
    for s in range(A_DIAG):
        e = jnp.exp2(b3 - c3[:, s:s + 1])
        col = jnp.sum(q3 * e, axis=-1, keepdims=True)
        ch = jnp.where(lane3 == s, col, ch)
    return jnp.where(tri >= 0, ch.reshape(A_CHUNK, LANES), 0.0).astype(BF16)


def _hgrn2_levels(a, q, k, b2, xr):
    c = A_CHUNK
    for level in A_LEVELS:
        half = level // 2
        shape4 = (c // level, 2, half, LANES)
        q4, k4, b4 = (t.reshape(shape4) for t in (q, k, b2))
        b_mid = b4[:, 0, half - 1:half]
        zero = jnp.zeros((c // level, half, LANES), F32)
        qu = q4[:, 1] * jnp.exp2(b4[:, 1] - b_mid)
        kl = k4[:, 0] * jnp.exp2(b_mid - b4[:, 0])
        ql = jnp.concatenate([zero, qu], axis=1).reshape(c, LANES).astype(BF16)
        kl = jnp.concatenate([kl, zero], axis=1).reshape(c, LANES).astype(BF16)
        al = lax.dot_general(ql, kl, NT_DIMS, preferred_element_type=F32)
        a = jnp.where(xr < half, a, al)
    return a


def _hgrn2_pair(p_ref, j, fillers, lbl_ref, og_ref, wo_ref, tril, tile, xr, tri, o_ref,
                q_ref, k_ref, b_ref, c_ref, v_ref, g_ref, cm_ref, a_ref, st_refs, oh_ref,
                *, lb_slots):
    gate_fillers, diag_fillers, level_fillers, rec_fillers = (list(f) for f in fillers)
    tt = p_ref.shape[0]
    n_ch = tt // A_CHUNK
    jobs = [(e, c) for e in range(2) for c in range(n_ch)]
    rows = lambda c: slice(c * A_CHUNK, (c + 1) * A_CHUNK)

    for job in gate_fillers:
        job()

    hi_parts, lo_parts = [], []
    for e in range(2):
        lg = lbl_ref[2 * j + e]
        ex = jnp.exp(lg - jnp.max(lg, axis=0, keepdims=True))
        lb = (jnp.sum(ex[:lb_slots], axis=0, keepdims=True)
              / jnp.sum(ex, axis=0, keepdims=True))
        lf2 = _hgrn2_gates(p_ref, e, lb, q_ref, k_ref, c_ref, v_ref, g_ref)
        hi = lf2.astype(BF16)
        lo = (lf2 - hi.astype(F32)).astype(BF16)
        hi_parts += [hi[rows(c)] for c in range(n_ch)]
        lo_parts += [lo[rows(c)] for c in range(n_ch)]
    b_all = (jnp.dot(tril, jnp.concatenate(hi_parts, axis=1), preferred_element_type=F32)
             + jnp.dot(tril, jnp.concatenate(lo_parts, axis=1), preferred_element_type=F32))
    for i, (e, c) in enumerate(jobs):
        b2 = b_all[:, i * LANES:(i + 1) * LANES]
        b_ref[e, rows(c), :] = b2
        c_ref[e, rows(c), :] = b2 - c_ref[e, rows(c), :]

    def run_fillers(pending, every):
        return pending.pop(0)() if pending and every else None

    for i, (e, c) in enumerate(jobs):
        cm_ref[i * A_CHUNK:(i + 1) * A_CHUNK, :] = _hgrn2_diag(
            q_ref[e, rows(c), :], b_ref[e, rows(c), :], c_ref[e, rows(c), :], tri)
        run_fillers(diag_fillers, True)
    for job in diag_fillers:
        job()
    ad_all = jnp.dot(cm_ref[...], tile, preferred_element_type=F32)

    for i, (e, c) in enumerate(jobs):
        a = _hgrn2_levels(ad_all[i * A_CHUNK:(i + 1) * A_CHUNK], q_ref[e, rows(c), :],
                          k_ref[e, rows(c), :], b_ref[e, rows(c), :], xr)
        a_ref[i] = a.astype(BF16)
        run_fillers(level_fillers, i % 2 == 1)
    for job in level_fillers:
        job()

    for c in range(n_ch):
        for e in range(2):
            st_ref = st_refs[e]
            q, k, b2, v = (r[e, rows(c), :] for r in (q_ref, k_ref, b_ref, v_ref))
            st = st_ref[j]
            b_last = b2[A_CHUNK - 1:A_CHUNK, :]
            q_in = (q * jnp.exp2(b2)).astype(BF16)
            o = (jnp.dot(a_ref[e * n_ch + c], v.astype(BF16), preferred_element_type=F32)
                 + lax.dot_general(q_in, st.astype(BF16), NT_DIMS, preferred_element_type=F32))
            k_out = (k * jnp.exp2(b_last - b2)).astype(BF16)
            st_ref[j] = st * jnp.exp2(b_last) + jnp.dot(v.T.astype(BF16), k_out,
                                                         preferred_element_type=F32)
            o = _rms_rows(o, og_ref[...]) * g_ref[e, rows(c), :]
            oh_ref[rows(c), e * A_DIM:(e + 1) * A_DIM] = o.astype(BF16)
        run_fillers(rec_fillers, True)
    for job in rec_fillers:
        job()
    o_ref[...] += jnp.dot(oh_ref[...], wo_ref[j], preferred_element_type=F32)


def _ffn_hgrn2_kernel(x_ref, fgain_ref, wg_ref, wu_ref, wfo_ref, gain_ref, w_ref, lbl_ref, og_ref,
                      wo_ref, tril_ref, tile_ref, o_ref, u_ref, un_ref, fh_ref, act_ref,
                      h_ref, pa_ref, pb_ref, q_ref, k_ref, b_ref, c_ref, v_ref, g_ref, cm_ref,
                      a_ref, st0_ref, st1_ref, oh_ref, *, lb_slots, tiles_per_seq):
    step = pl.program_id(0)

    @pl.when(step == 0)
    def _():
        u_ref[...] = jnp.zeros_like(u_ref)

    @pl.when(lax.rem(jnp.maximum(step - 1, 0), tiles_per_seq) == 0)
    def _():
        st0_ref[...] = jnp.zeros_like(st0_ref)
        st1_ref[...] = jnp.zeros_like(st1_ref)

    ffn_jobs = _ffn_jobs(x_ref, fgain_ref, wg_ref, wu_ref, wfo_ref, fh_ref, act_ref, un_ref)
    x = u_ref[...]
    o_ref[...] = x
    h_ref[...] = _rms_rows(x, gain_ref[...]).astype(BF16)
    tril = tril_ref[...]
    tile = tile_ref[...]
    row = lax.broadcasted_iota(jnp.int32, (A_CHUNK, LANES), 0)
    lane = lax.broadcasted_iota(jnp.int32, (A_CHUNK, LANES), 1)
    xr = jnp.bitwise_xor(row, lane)
    tri = jnp.bitwise_and(row, A_DIAG - 1) - lane
    n_pairs = wo_ref.shape[0]
    part_width = w_ref.shape[1] // 4

    def projection_jobs(dst_ref, pair):
        def job(part):
            def run():
                src = part * part_width + pair * 2 * A_DIM
                dst_ref[:, part * 2 * A_DIM:(part + 1) * 2 * A_DIM] = jnp.dot(
                    h_ref[...], w_ref[:, src:src + 2 * A_DIM], preferred_element_type=F32)
            return run
        return [job(part) for part in range(4)]

    common = dict(lbl_ref=lbl_ref, og_ref=og_ref, wo_ref=wo_ref, tril=tril, tile=tile, xr=xr,
                  tri=tri, o_ref=o_ref, q_ref=q_ref, k_ref=k_ref, b_ref=b_ref, c_ref=c_ref,
                  v_ref=v_ref, g_ref=g_ref, cm_ref=cm_ref, a_ref=a_ref,
                  st_refs=(st0_ref, st1_ref), oh_ref=oh_ref, lb_slots=lb_slots)

    for job in projection_jobs(pa_ref, 0):
        job()

    bufs = (pa_ref, pb_ref)
    ffn_jobs.pop(0)()
    per_pair = -(-len(ffn_jobs) // n_pairs)
    for j in range(n_pairs):
        ahead = projection_jobs(bufs[(j + 1) % 2], j + 1) if j + 1 < n_pairs else []
        mine = ffn_jobs[j * per_pair:(j + 1) * per_pair]
        diag_fillers = mine[1:2] + ahead[:2] + mine[2:3] + ahead[2:]
        _hgrn2_pair(bufs[j % 2], j, (mine[:1], diag_fillers, mine[3:], []), **common)
    u_ref[...] = un_ref[...]


def _ffn_hgrn2_layer(x2, seq, fgain, f_w_in, f_w_out, gain, w_in, lb_logits, layer, out_gain,
                     w_out, *, tt=512):
    t, d = x2.shape
    d_ff = f_w_out.shape[0]
    n_tiles = t // tt
    n_slots = lb_logits.shape[0]
    n_pairs = A_HEADS // 2
    n_jobs = 2 * (tt // A_CHUNK)
    wo_pairs = w_out.reshape(n_pairs, 2 * A_DIM, d).astype(BF16)
    lbl = lb_logits.reshape(n_slots, A_HEADS, A_DIM).transpose(1, 0, 2)
    tril = jnp.asarray(np.tril(np.ones((A_CHUNK, A_CHUNK), np.float32)), BF16)
    idx = np.arange(LANES)
    tile = jnp.asarray((idx[:, None] == idx[None, :] % A_DIAG).astype(np.float32), BF16)
    head_tile = lambda dtype: pltpu.VMEM((2, tt, A_DIM), dtype)
    return pl.pallas_call(
        functools.partial(_ffn_hgrn2_kernel, lb_slots=layer + 1, tiles_per_seq=seq // tt),
        grid=(n_tiles + 1,),
        in_specs=[
            pl.BlockSpec((tt, d), lambda s: (jnp.minimum(s, n_tiles - 1), 0)),
            _const_spec((1, d)),
            _const_spec((d, d_ff)),
            _const_spec((d, d_ff)),
            _const_spec((d_ff, d)),
            _const_spec((1, d)),
            _const_spec(w_in.shape),
            _const_spec((A_HEADS, n_slots, A_DIM)),
            _const_spec((1, A_DIM)),
            _const_spec((n_pairs, 2 * A_DIM, d)),
            _const_spec((A_CHUNK, A_CHUNK)),
            _const_spec((LANES, LANES)),
        ],
        out_specs=pl.BlockSpec((tt, d), lambda s: (jnp.maximum(s - 1, 0), 0)),
        out_shape=jax.ShapeDtypeStruct((t, d), F32),
        scratch_shapes=[
            pltpu.VMEM((tt, d), F32),
            pltpu.VMEM((tt, d), F32),
            pltpu.VMEM((tt, d), BF16),
            pltpu.VMEM((tt, d_ff), BF16),
            pltpu.VMEM((tt, d), BF16),
            pltpu.VMEM((tt, 8 * A_DIM), F32),
            pltpu.VMEM((tt, 8 * A_DIM), F32),
            head_tile(F32), head_tile(F32), head_tile(F32),
            head_tile(F32), head_tile(F32), head_tile(F32),
            pltpu.VMEM((n_jobs * A_CHUNK, LANES), BF16),
            pltpu.VMEM((n_jobs, A_CHUNK, LANES), BF16),
            pltpu.VMEM((n_pairs, A_DIM, A_DIM), F32),
            pltpu.VMEM((n_pairs, A_DIM, A_DIM), F32),
            pltpu.VMEM((tt, 2 * A_DIM), BF16),
        ],
        compiler_params=pltpu.CompilerParams(
            dimension_semantics=("arbitrary",), vmem_limit_bytes=VMEM_LIMIT_FUSED),
        name="swiglu_hgrn2_layer",
    )(x2, fgain.reshape(1, d), *_ffn_weights(f_w_in, f_w_out), gain.reshape(1, d),
      w_in.astype(BF16), lbl, out_gain.reshape(1, A_DIM), wo_pairs, tril, tile)


def _proj_kernel(x_ref, gain_ref, w_ref, hg_ref, *refs, n_normed, scale):
    out_refs, hs_ref = refs[:-1], refs[-1]
    tm = x_ref.shape[1]
    width = out_refs[0].shape[-1]
    hf = _rms_rows(x_ref[0], gain_ref[...])
    n_slabs = hs_ref.shape[0]
    for s in range(n_slabs):
        hs_ref[s] = hf[:, s * LANES:(s + 1) * LANES]
    lhs = {}
    for dil in sorted(set(B_DILATIONS)):
        if dil == 1:
            lhs[dil] = hf.astype(BF16)
            continue
        slabs = [jnp.concatenate([hs_ref[s, pl.ds(r, tm // dil, stride=dil), :]
                                  for r in range(dil)], axis=0).astype(BF16)
                 for s in range(n_slabs)]
        lhs[dil] = jnp.concatenate(slabs, axis=1)
    for j, o_ref in enumerate(out_refs):
        kind, g = divmod(j, B_GROUPS)
        dil = B_DILATIONS[g]
        for c0 in range(0, width, MXU_COLS):
            col = j * width + c0
            t2 = jnp.dot(lhs[dil], w_ref[:, col:col + MXU_COLS], preferred_element_type=F32)
            for c1 in range(c0, c0 + MXU_COLS, LANES):
                t = t2[:, c1 - c0:c1 - c0 + LANES]
                if kind < n_normed:
                    t = _head_rms_pair(t, hg_ref[g:g + 1, c1:c1 + LANES]) * scale
                o_ref[0, :, :, c1:c1 + LANES] = t.astype(BF16).reshape(dil, tm // dil, LANES)


def _project(x, gain, w, head_gain, *, n_kinds, n_normed, scale, tm=512):
    bsz, seq, d = x.shape
    n_out = n_kinds * B_GROUPS
    width = w.shape[1] // n_out
    hg = jnp.tile(head_gain, (1, B_HEADS))
    dils = [B_DILATIONS[j % B_GROUPS] for j in range(n_out)]
    outs = pl.pallas_call(
        functools.partial(_proj_kernel, n_normed=n_normed, scale=scale),
        grid=(bsz, seq // tm),
        in_specs=[
            pl.BlockSpec((1, tm, d), lambda b, i: (b, i, 0)),
            _const_spec((1, d)),
            _const_spec((d, n_out * width)),
            _const_spec((B_GROUPS, width)),
        ],
        out_specs=[pl.BlockSpec((1, dil, tm // dil, width), lambda b, i: (b, 0, i, 0))
                   for dil in dils],
        out_shape=[jax.ShapeDtypeStruct((bsz, dil, seq // dil, width), BF16) for dil in dils],
        scratch_shapes=[pltpu.VMEM((d // LANES, tm, LANES), F32)],
        compiler_params=pltpu.CompilerParams(
            dimension_semantics=("arbitrary", "arbitrary"), vmem_limit_bytes=VMEM_LIMIT),
        name="norm_projection",
    )(x, gain.reshape(1, d), w.astype(BF16), hg)
    return outs


def _t5_bucket(dist):
    dist = np.asarray(dist, np.int32)
    max_exact = NUM_BUCKETS // 2
    large = max_exact + (np.log(np.maximum(dist, 1) / max_exact)
                         / math.log(MAX_DISTANCE / max_exact) * (NUM_BUCKETS - max_exact)).astype(np.int32)
    large = np.minimum(large, NUM_BUCKETS - 1)
    return np.where(dist < max_exact, dist, large).astype(np.int32)


def _band_bias(rel_bias, group):
    dil = B_DILATIONS[group]
    wd = B_WINDOWS[group] // dil
    i = np.arange(B_BLOCK)[:, None]
    j = np.arange(2 * B_BLOCK)[None, :]
    dist = i + wd - j
    band = (dist >= 0) & (dist <= wd)
    bucket = _t5_bucket(np.clip(dist, 0, wd) * dil)
    ids = jnp.asarray(np.where(band, bucket, -1), jnp.int32)
    onehot = (ids[None] == jnp.arange(NUM_BUCKETS, dtype=jnp.int32)[:, None, None]).astype(F32)
    table = rel_bias[:, group * B_HEADS:(group + 1) * B_HEADS].astype(F32)
    bias = jnp.einsum("bh,bij->hij", table, onehot,
                      precision=lax.Precision.HIGHEST) * LOG2E
    pair_shape = (B_HEADS // 2, 2 * B_BLOCK, 2 * B_BLOCK)
    tables = []
    for keep in (band & (j >= B_BLOCK), band):
        tables.append((bias + jnp.asarray(np.where(keep, 0.0, NEG), F32)[None]).reshape(pair_shape))
    return jnp.stack(tables)


def _stat_lane(head):
    return (head % 2) * (LANES // 2) + head // 2


def _attn_kernel(q_ref, kp_ref, kc_ref, vp_ref, vc_ref, bias_ref, o_ref, m_ref, l_ref):
    nq = B_BLOCK
    half = LANES // 2
    n_tiles = q_ref.shape[2] // nq
    n_pairs = B_HEADS // 2
    first_table = jnp.where(pl.program_id(2) == 0, 0, 1)
    lo_q = _half_lane_mask((nq, LANES))
    lo_k = _half_lane_mask((2 * nq, LANES))
    lane = lax.broadcasted_iota(jnp.int32, (nq, LANES), 1)
    stat_lane = jnp.bitwise_and(lane, half - 1)
    ones_lo = jnp.where(lo_k, 1.0, 0.0).astype(BF16)
    ones_hi = jnp.where(lo_k, 0.0, 1.0).astype(BF16)

    def keys_values(prev_ref, cur_ref, t, sl):
        if t == 0:
            return jnp.concatenate([prev_ref[0, 0, :, sl], cur_ref[0, 0, :nq, sl]], axis=0)
        return cur_ref[0, 0, (t - 1) * nq:(t + 1) * nq, sl]

    def scores(job):
        t, pair = divmod(job, n_pairs)
        sl = slice(pair * LANES, (pair + 1) * LANES)
        qp = q_ref[0, 0, t * nq:(t + 1) * nq, sl]
        kk = keys_values(kp_ref, kc_ref, t, sl)
        zq = jnp.zeros_like(qp)
        qq = jnp.concatenate([jnp.where(lo_q, qp, zq), jnp.where(lo_q, zq, qp)], axis=0)
        table = first_table if t == 0 else 1
        return lax.dot_general(qq, kk, NT_DIMS, preferred_element_type=F32) + bias_ref[table, pair]

    s_next = scores(0)
    for job in range(n_tiles * n_pairs):
        t, pair = divmod(job, n_pairs)
        sl = slice(pair * LANES, (pair + 1) * LANES)
        rows = slice(t * nq, (t + 1) * nq)
        if pair == 0:
            m_tile = jnp.zeros((nq, LANES), F32)
            l_tile = jnp.ones((nq, LANES), F32)
        s = s_next
        if job + 1 < n_tiles * n_pairs:
            s_next = scores(job + 1)
        vv = keys_values(vp_ref, vc_ref, t, sl)
        zv = jnp.zeros_like(vv)
        m = jnp.max(s, axis=-1, keepdims=True)
        p = jnp.exp2(s - m).astype(BF16)
        pk = jnp.concatenate([p[:nq], p[nq:]], axis=1)
        rhs = jnp.concatenate(
            [jnp.concatenate([jnp.where(lo_k, vv, zv), ones_lo], axis=1),
             jnp.concatenate([jnp.where(lo_k, zv, vv), ones_hi], axis=1)], axis=0)
        ov = jnp.dot(pk, rhs, preferred_element_type=F32)
        o_ref[0, 0, rows, sl] = ov[:, :LANES].astype(BF16)
        here = stat_lane == pair
        m_tile = jnp.where(here, jnp.where(lo_q, m[:nq], m[nq:]), m_tile)
        l_tile = jnp.where(here, ov[:, LANES:], l_tile)
        if pair == n_pairs - 1:
            m_ref[0, 0, rows, :] = m_tile
            l_ref[0, 0, rows, :] = l_tile


def _attn_group(q, k, v, rel_bias, group, *, tq=1024):
    bsz, dil, n_sub, width = q.shape
    tq = min(tq, n_sub)
    per_block = tq // B_BLOCK
    cur = lambda w: pl.BlockSpec((1, 1, tq, w), lambda b, r, i: (b, r, i, 0))
    prev = lambda w: pl.BlockSpec((1, 1, B_BLOCK, w),
                                  lambda b, r, i: (b, r, jnp.maximum(i * per_block - 1, 0), 0))
    stats = jax.ShapeDtypeStruct((bsz, dil, n_sub, LANES), F32)
    return pl.pallas_call(
        _attn_kernel,
        grid=(bsz, dil, n_sub // tq),
        in_specs=[cur(width), prev(width), cur(width), prev(width), cur(width),
                  _const_spec((2, B_HEADS // 2, 2 * B_BLOCK, 2 * B_BLOCK))],
        out_specs=[cur(width), cur(LANES), cur(LANES)],
        out_shape=[jax.ShapeDtypeStruct((bsz, dil, n_sub, width), BF16), stats, stats],
        compiler_params=pltpu.CompilerParams(
            dimension_semantics=("arbitrary", "arbitrary", "arbitrary"),
            vmem_limit_bytes=VMEM_LIMIT),
        name=f"dilated_attention_g{group}",
    )(q, k, k, v, v, _band_bias(rel_bias, group))


def _merge_jobs(x_ref, o_refs, m_refs, l_refs, ex_ref, wo_ref, slab_ref, lslab_ref, w_ref, acc_ref,
                dst_ref):
    tm, d = x_ref.shape
    n_slabs = slab_ref.shape[0]

    def stat_rows(l_ref):
        dil = l_ref.shape[1]
        if dil == 1:
            return l_ref[0, 0]
        for r in range(dil):
            lslab_ref[pl.ds(r, tm // dil, stride=dil), :] = l_ref[0, r]
        return lslab_ref[...]

    def out_rows(o_ref):
        dil = o_ref.shape[1]
        if dil == 1:
            return o_ref[0, 0].astype(F32)
        for s in range(n_slabs):
            for r in range(dil):
                slab_ref[s, pl.ds(r, tm // dil, stride=dil), :] = (
                    o_ref[0, r, :, s * LANES:(s + 1) * LANES].astype(F32))
        return jnp.concatenate([slab_ref[s] for s in range(n_slabs)], axis=1)

    def weights():
        ms = [stat_rows(m_ref) for m_ref in m_refs]
        ls = [stat_rows(l_ref) for l_ref in l_refs]
        top = functools.reduce(jnp.maximum, ms)
        es = [jnp.exp2(m - top) for m in ms]
        inv = 1.0 / sum(e * l for e, l in zip(es, ls))
        for g, e in enumerate(es):
            w_ref[g] = (e * inv).astype(BF16)

    def group(g):
        def run():
            w = jnp.dot(w_ref[g], ex_ref[...], preferred_element_type=F32)
            term = w * out_rows(o_refs[g])
            acc_ref[...] = term if g == 0 else acc_ref[...] + term
        return run

    def output(c0):
        def run():
            sl = slice(c0, c0 + MXU_COLS)
            y = jnp.dot(acc_ref[...].astype(BF16), wo_ref[:, sl], preferred_element_type=F32)
            dst_ref[:, sl] = x_ref[:, sl] + y
        return run

    return ([weights] + [group(g) for g in range(len(o_refs))]
            + [output(c0) for c0 in range(0, d, MXU_COLS)])


def _merge_ffn_kernel(x_ref, o0_ref, o1_ref, o2_ref, m0_ref, m1_ref, m2_ref, l0_ref, l1_ref, l2_ref,
                      ex_ref, wo_ref, fgain_ref, wg_ref, wu_ref, wfo_ref, out_ref,
                      u_ref, un_ref, slab_ref, lslab_ref, w_ref, acc_ref, fh_ref, act_ref):
    @pl.when(pl.program_id(0) == 0)
    def _():
        u_ref[...] = jnp.zeros_like(u_ref)

    merge_jobs = _merge_jobs(x_ref, (o0_ref, o1_ref, o2_ref), (m0_ref, m1_ref, m2_ref),
                             (l0_ref, l1_ref, l2_ref), ex_ref, wo_ref, slab_ref, lslab_ref,
                             w_ref, acc_ref, un_ref)
    ffn_jobs = _ffn_jobs(u_ref, fgain_ref, wg_ref, wu_ref, wfo_ref, fh_ref, act_ref, out_ref)
    per_slot = -(-len(ffn_jobs) // len(merge_jobs))
    for i, job in enumerate(merge_jobs):
        job()
        for ffn_job in ffn_jobs[i * per_slot:(i + 1) * per_slot]:
            ffn_job()
    u_ref[...] = un_ref[...]


def _merge_ffn(x2, seq, os, ms, ls, w_o, fgain, f_w_in, f_w_out, *, tm=512):
    t, d = x2.shape
    d_ff = f_w_out.shape[0]
    width = os[0].shape[-1]
    n_tiles = t // tm
    tiles_per_seq = seq // tm
    expand = np.zeros((LANES, width), np.float32)
    for h in range(B_HEADS):
        expand[_stat_lane(h), h * B_HEAD_DIM:(h + 1) * B_HEAD_DIM] = 1.0

    def tile(s):
        return jnp.minimum(s, n_tiles - 1)

    grouped = lambda a: pl.BlockSpec(
        (1, a.shape[1], tm // a.shape[1], a.shape[3]),
        lambda s: (tile(s) // tiles_per_seq, 0, lax.rem(tile(s), tiles_per_seq), 0))
    return pl.pallas_call(
        _merge_ffn_kernel,
        grid=(n_tiles + 1,),
        in_specs=[pl.BlockSpec((tm, d), lambda s: (tile(s), 0))]
                 + [grouped(a) for a in (*os, *ms, *ls)]
                 + [_const_spec((LANES, width)), _const_spec((width, d)), _const_spec((1, d)),
                    _const_spec((d, d_ff)), _const_spec((d, d_ff)), _const_spec((d_ff, d))],
        out_specs=pl.BlockSpec((tm, d), lambda s: (jnp.maximum(s - 1, 0), 0)),
        out_shape=jax.ShapeDtypeStruct((t, d), F32),
        scratch_shapes=[
            pltpu.VMEM((tm, d), F32),
            pltpu.VMEM((tm, d), F32),
            pltpu.VMEM((width // LANES, tm, LANES), F32),
            pltpu.VMEM((tm, LANES), F32),
            pltpu.VMEM((len(os), tm, LANES), BF16),
            pltpu.VMEM((tm, width), F32),
            pltpu.VMEM((tm, d), BF16),
            pltpu.VMEM((tm, d_ff), BF16),
        ],
        compiler_params=pltpu.CompilerParams(
            dimension_semantics=("arbitrary",), vmem_limit_bytes=VMEM_LIMIT_FUSED),
        name="merge_swiglu_half_step",
    )(x2, *os, *ms, *ls, jnp.asarray(expand, BF16), w_o.astype(BF16), fgain.reshape(1, d),
      *_ffn_weights(f_w_in, f_w_out))


def kernel(x, norm_gain, ffn_w_in, ffn_w_out, a_w_in, a_lb_logits, a_out_gain, a_w_out,
           kv_norm, w_kv, k_gain, b_w_q, b_q_gain, b_w_o, rel_bias):
    bsz, seq, d = x.shape
    depth = norm_gain.shape[0]
    n_a = a_w_in.shape[0]
    t = bsz * seq
    ks = vs = None
    for layer in range(depth):
        if layer == n_a:
            outs = _project(x, kv_norm, w_kv, k_gain, n_kinds=2, n_normed=1, scale=1.0)
            ks, vs = outs[:B_GROUPS], outs[B_GROUPS:]
        if layer < n_a:
            x = _ffn_hgrn2_layer(x.reshape(t, d), seq, norm_gain[layer, 0], ffn_w_in[layer, 0],
                                 ffn_w_out[layer, 0], norm_gain[layer, 1], a_w_in[layer],
                                 a_lb_logits, layer, a_out_gain[layer],
                                 a_w_out[layer]).reshape(bsz, seq, d)
            x = _ffn(x.reshape(t, d), norm_gain[layer, 2], ffn_w_in[layer, 1],
                     ffn_w_out[layer, 1]).reshape(bsz, seq, d)
        else:
            j = layer - n_a
            x = _ffn(x.reshape(t, d), norm_gain[layer, 0], ffn_w_in[layer, 0],
                     ffn_w_out[layer, 0]).reshape(bsz, seq, d)
            qs = _project(x, norm_gain[layer, 1], b_w_q[j], b_q_gain[j],
                          n_kinds=1, n_normed=1, scale=B_HEAD_DIM ** -0.5 * LOG2E)
            parts = [_attn_group(qs[g], ks[g], vs[g], rel_bias, g) for g in range(B_GROUPS)]
            os, ms, ls = zip(*parts)
            x = _merge_ffn(x.reshape(t, d), seq, os, ms, ls, b_w_o[j], norm_gain[layer, 2],
                           ffn_w_in[layer, 1], ffn_w_out[layer, 1]).reshape(bsz, seq, d)
    return x
```

```python
import functools
import math
from typing import NamedTuple

import jax
import jax.numpy as jnp
import numpy as np
from jax import lax
from jax.experimental import pallas as pl
from jax.experimental.pallas import tpu as pltpu

F32 = jnp.float32
BF16 = jnp.bfloat16

EPS = 1e-6
LOG2E = math.log2(math.e)
F32_TINY = float(np.finfo(np.float32).tiny)
LANES = 128
SUBLANES = 8
MXU_COLS = 256
VMEM_LIMIT = 56 * 1024 * 1024
VMEM_LIMIT_FUSED = 60 * 1024 * 1024

A_HEADS = 8
A_DIM = 128
A_CHUNK = 128
A_DIAG = SUBLANES
A_LEVELS = (16, 32, 64, 128)

B_WINDOWS = (128, 512, 2048)
B_DILATIONS = (1, 4, 16)
B_GROUPS = 3
B_HEADS = 16
B_HEAD_DIM = 64
B_BLOCK = 128
NUM_BUCKETS = 32
MAX_DISTANCE = 2048
NEG = -1e30

NT_DIMS = (((1,), (1,)), ((), ()))


def _const_spec(shape):
    nd = len(shape)
    return pl.BlockSpec(shape, lambda *_: (0,) * nd, pipeline_mode=pl.Buffered(1))


def _sigmoid(x):
    return 1.0 / (1.0 + jnp.exp(-x))


def _rms_rows(x, gain):
    ms = jnp.mean(x * x, axis=-1, keepdims=True)
    return x * lax.rsqrt(ms + EPS) * gain


def _half_lane_mask(shape):
    return lax.broadcasted_iota(jnp.int32, shape, len(shape) - 1) < (LANES // 2)


def _head_rms_pair(t, gain_row):
    lo = _half_lane_mask(t.shape)
    sq = t * t
    s_lo = jnp.sum(jnp.where(lo, sq, 0.0), axis=-1, keepdims=True)
    s_hi = jnp.sum(jnp.where(lo, 0.0, sq), axis=-1, keepdims=True)
    ms = jnp.where(lo, s_lo, s_hi) * (1.0 / B_HEAD_DIM)
    return t * lax.rsqrt(ms + EPS) * gain_row


class _SideCasts(NamedTuple):
    in_specs: list
    out_specs: list
    out_shapes: list
    args: list


def _cast_block_rows(rows, steps):
    tile = 2 * SUBLANES
    br = tile
    while rows % br or rows // br > steps:
        br += tile
    return br


def _side_casts(casts, n_steps, step_of):
    side = _SideCasts([], [], [], [])
    for arr, lead in casts:
        rows, cols = arr.shape[-2:]
        br = _cast_block_rows(rows, n_steps)
        last = rows // br - 1

        def block(*g, last=last):
            return jnp.minimum(step_of(*g), last)

        side.in_specs.append(pl.BlockSpec((None,) * len(lead) + (br, cols),
                                          lambda *g, lead=lead, block=block: (*lead, block(*g), 0)))
        side.out_specs.append(pl.BlockSpec((br, cols), lambda *g, block=block: (block(*g), 0)))
        side.out_shapes.append(jax.ShapeDtypeStruct((rows, cols), BF16))
        side.args.append(arr)
    return side


def _split_refs(refs, n_casts, n_out):
    n_mid = n_out + n_casts
    return refs[:n_casts], refs[n_casts:n_casts + n_mid], refs[n_casts + n_mid:]


def _run_casts(cast_in, cast_out):
    for src, dst in zip(cast_in, cast_out):
        dst[...] = src[...].astype(BF16)


def _ffn_jobs(x_ref, gain_ref, wi_ref, wo_ref, h_ref, act_ref, dst_ref):
    d_ff, d = wo_ref.shape

    def norm():
        h_ref[...] = _rms_rows(x_ref[...], gain_ref[...]).astype(BF16)

    def hidden(c0):
        def run():
            sl = slice(c0, c0 + MXU_COLS)
            h = h_ref[...]
            g = jnp.dot(h, wi_ref[:, sl], preferred_element_type=F32)
            u = jnp.dot(h, wi_ref[:, d_ff + c0:d_ff + c0 + MXU_COLS], preferred_element_type=F32)
            act_ref[:, sl] = (g * _sigmoid(g) * u).astype(BF16)
        return run

    def output(c0):
        def run():
            sl = slice(c0, c0 + MXU_COLS)
            y = jnp.dot(act_ref[...], wo_ref[:, sl], preferred_element_type=F32)
            dst_ref[:, sl] = x_ref[:, sl] + 0.5 * y
        return run

    return ([norm] + [hidden(c0) for c0 in range(0, d_ff, MXU_COLS)]
            + [output(c0) for c0 in range(0, d, MXU_COLS)])


def _ffn_kernel(x_ref, gain_ref, wi_ref, wo_ref, *refs, n_casts):
    cast_in, (o_ref, *cast_out), (h_ref, act_ref) = _split_refs(refs, n_casts, 1)
    _run_casts(cast_in, cast_out)
    for job in _ffn_jobs(x_ref, gain_ref, wi_ref, wo_ref, h_ref, act_ref, o_ref):
        job()


def _ffn(x2, gain, w_in, w_out, casts=(), *, tm=1024):
    t, d = x2.shape
    d_ff = w_out.shape[0]
    side = _side_casts(casts, t // tm, lambda i: i)
    out, *cast = pl.pallas_call(
        functools.partial(_ffn_kernel, n_casts=len(casts)),
        grid=(t // tm,),
        in_specs=[
            pl.BlockSpec((tm, d), lambda i: (i, 0)),
            _const_spec((1, d)),
            _const_spec((d, 2 * d_ff)),
            _const_spec((d_ff, d)),
        ] + side.in_specs,
        out_specs=[pl.BlockSpec((tm, d), lambda i: (i, 0))] + side.out_specs,
        out_shape=[jax.ShapeDtypeStruct((t, d), F32)] + side.out_shapes,
        scratch_shapes=[pltpu.VMEM((tm, d), BF16), pltpu.VMEM((tm, d_ff), BF16)],
        compiler_params=pltpu.CompilerParams(
            dimension_semantics=("arbitrary",), vmem_limit_bytes=VMEM_LIMIT),
        name="swiglu_half_step",
    )(x2, gain.reshape(1, d), w_in, w_out, *side.args)
    return out, cast


def _hgrn2_gates(p_ref, e, lb, q_ref, k_ref, c_ref, v_ref, g_ref):
    qp, fp, v, gp = (p_ref[:, (2 * i + e) * A_DIM:(2 * i + e + 1) * A_DIM] for i in range(4))
    z = jnp.exp(-jnp.abs(fp))
    r = 1.0 / (1.0 + z)
    pos = fp >= 0
    sig = jnp.where(pos, r, z * r)
    one_minus_sig = jnp.where(pos, z * r, r)
    f = jnp.maximum(lb + (1.0 - lb) * sig, F32_TINY)
    k = (1.0 - lb) * one_minus_sig
    q_ref[e] = qp * _sigmoid(qp)
    k_ref[e] = k
    c_ref[e] = jnp.log(k) * LOG2E
    v_ref[e] = v
    g_ref[e] = gp * _sigmoid(gp)
    return jnp.log(f) * LOG2E


def _hgrn2_diag(q, b2, cs, tri):
    nb = A_CHUNK // A_DIAG
    q3, b3, c3 = (t.reshape(nb, A_DIAG, LANES) for t in (q, b2, cs))
    lane3 = lax.broadcasted_iota(jnp.int32, (nb, A_DIAG, LANES), 2)
    ch = jnp.zeros((nb, A_DIAG, LANES), F32)
    for s in range(A_DIAG):
        e = jnp.exp2(b3 - c3[:, s:s + 1])
        col = jnp.sum(q3 * e, axis=-1, keepdims=True)
        ch = jnp.where(lane3 == s, col, ch)
    return jnp.where(tri >= 0, ch.reshape(A_CHUNK, LANES), 0.0).astype(BF16)


def _hgrn2_levels(a, q, k, b2, xr):
    c = A_CHUNK
    for level in A_LEVELS:
        half = level // 2
        shape4 = (c // level, 2, half, LANES)
        q4, k4, b4 = (t.reshape(shape4) for t in (q, k, b2))
        b_mid = b4[:, 0, half - 1:half]
        zero = jnp.zeros((c // level, half, LANES), F32)
        qu = q4[:, 1] * jnp.exp2(b4[:, 1] - b_mid)
        kl = k4[:, 0] * jnp.exp2(b_mid - b4[:, 0])
        ql = jnp.concatenate([zero, qu], axis=1).reshape(c, LANES).astype(BF16)
        kl = jnp.concatenate([kl, zero], axis=1).reshape(c, LANES).astype(BF16)
        al = lax.dot_general(ql, kl, NT_DIMS, preferred_element_type=F32)
        a = jnp.where(xr < half, a, al)
    return a


def _hgrn2_pair(p_ref, j, fillers, lbl_ref, og_ref, wo_ref, tril, tile, xr, tri, o_ref,
                q_ref, k_ref, b_ref, c_ref, v_ref, g_ref, cm_ref, a_ref, st_refs, oh_ref,
                *, lb_slots):
    diag_fillers, level_fillers, rec_fillers = (list(f) for f in fillers)
    tt = p_ref.shape[0]
    n_ch = tt // A_CHUNK
    jobs = [(e, c) for e in range(2) for c in range(n_ch)]
    rows = lambda c: slice(c * A_CHUNK, (c + 1) * A_CHUNK)

    hi_parts, lo_parts = [], []
    for e in range(2):
        lg = lbl_ref[2 * j + e]
        ex = jnp.exp(lg - jnp.max(lg, axis=0, keepdims=True))
        lb = (jnp.sum(ex[:lb_slots], axis=0, keepdims=True)
              / jnp.sum(ex, axis=0, keepdims=True))
        lf2 = _hgrn2_gates(p_ref, e, lb, q_ref, k_ref, c_ref, v_ref, g_ref)
        hi = lf2.astype(BF16)
        lo = (lf2 - hi.astype(F32)).astype(BF16)
        hi_parts += [hi[rows(c)] for c in range(n_ch)]
        lo_parts += [lo[rows(c)] for c in range(n_ch)]
    b_all = (jnp.dot(tril, jnp.concatenate(hi_parts, axis=1), preferred_element_type=F32)
             + jnp.dot(tril, jnp.concatenate(lo_parts, axis=1), preferred_element_type=F32))
    for i, (e, c) in enumerate(jobs):
        b2 = b_all[:, i * LANES:(i + 1) * LANES]
        b_ref[e, rows(c), :] = b2
        c_ref[e, rows(c), :] = b2 - c_ref[e, rows(c), :]

    def run_fillers(pending, every):
        return pending.pop(0)() if pending and every else None

    for i, (e, c) in enumerate(jobs):
        cm_ref[i * A_CHUNK:(i + 1) * A_CHUNK, :] = _hgrn2_diag(
            q_ref[e, rows(c), :], b_ref[e, rows(c), :], c_ref[e, rows(c), :], tri)
        run_fillers(diag_fillers, True)
    for job in diag_fillers:
        job()
    ad_all = jnp.dot(cm_ref[...], tile, preferred_element_type=F32)

    for i, (e, c) in enumerate(jobs):
        a = _hgrn2_levels(ad_all[i * A_CHUNK:(i + 1) * A_CHUNK], q_ref[e, rows(c), :],
                          k_ref[e, rows(c), :], b_ref[e, rows(c), :], xr)
        a_ref[i] = a.astype(BF16)
        run_fillers(level_fillers, i % 2 == 1)
    for job in level_fillers:
        job()

    for c in range(n_ch):
        for e in range(2):
            st_ref = st_refs[e]
            q, k, b2, v = (r[e, rows(c), :] for r in (q_ref, k_ref, b_ref, v_ref))
            st = st_ref[j]
            b_last = b2[A_CHUNK - 1:A_CHUNK, :]
            q_in = (q * jnp.exp2(b2)).astype(BF16)
            o = (jnp.dot(a_ref[e * n_ch + c], v.astype(BF16), preferred_element_type=F32)
                 + lax.dot_general(q_in, st.astype(BF16), NT_DIMS, preferred_element_type=F32))
            k_out = (k * jnp.exp2(b_last - b2)).astype(BF16)
            st_ref[j] = st * jnp.exp2(b_last) + jnp.dot(v.T.astype(BF16), k_out,
                                                         preferred_element_type=F32)
            o = _rms_rows(o, og_ref[...]) * g_ref[e, rows(c), :]
            oh_ref[rows(c), e * A_DIM:(e + 1) * A_DIM] = o.astype(BF16)
        run_fillers(rec_fillers, True)
    for job in rec_fillers:
        job()
    o_ref[...] += jnp.dot(oh_ref[...], wo_ref[j], preferred_element_type=F32)


def _ffn_hgrn2_kernel(x_ref, fgain_ref, wfi_ref, wfo_ref, gain_ref, w_ref, lbl_ref, og_ref,
                      wo_ref, tril_ref, tile_ref, *refs, lb_slots, tiles_per_seq, n_casts):
    cast_in, (o_ref, *cast_out), scratch = _split_refs(refs, n_casts, 1)
    (u_ref, un_ref, fh_ref, act_ref, h_ref, pa_ref, pb_ref, q_ref, k_ref, b_ref, c_ref, v_ref,
     g_ref, cm_ref, a_ref, st0_ref, st1_ref, oh_ref) = scratch
    _run_casts(cast_in, cast_out)
    step = pl.program_id(0)

    @pl.when(step == 0)
    def _():
        u_ref[...] = jnp.zeros_like(u_ref)

    @pl.when(lax.rem(jnp.maximum(step - 1, 0), tiles_per_seq) == 0)
    def _():
        st0_ref[...] = jnp.zeros_like(st0_ref)
        st1_ref[...] = jnp.zeros_like(st1_ref)

    ffn_jobs = _ffn_jobs(x_ref, fgain_ref, wfi_ref, wfo_ref, fh_ref, act_ref, un_ref)
    x = u_ref[...]
    o_ref[...] = x
    h_ref[...] = _rms_rows(x, gain_ref[...]).astype(BF16)
    tril = tril_ref[...]
    tile = tile_ref[...]
    row = lax.broadcasted_iota(jnp.int32, (A_CHUNK, LANES), 0)
    lane = lax.broadcasted_iota(jnp.int32, (A_CHUNK, LANES), 1)
    xr = jnp.bitwise_xor(row, lane)
    tri = jnp.bitwise_and(row, A_DIAG - 1) - lane
    n_pairs = wo_ref.shape[0]
    part_width = w_ref.shape[1] // 4

    def projection_jobs(dst_ref, pair):
        def job(part):
            def run():
                src = part * part_width + pair * 2 * A_DIM
                dst_ref[:, part * 2 * A_DIM:(part + 1) * 2 * A_DIM] = jnp.dot(
                    h_ref[...], w_ref[:, src:src + 2 * A_DIM], preferred_element_type=F32)
            return run
        return [job(part) for part in range(4)]

    common = dict(lbl_ref=lbl_ref, og_ref=og_ref, wo_ref=wo_ref, tril=tril, tile=tile, xr=xr,
                  tri=tri, o_ref=o_ref, q_ref=q_ref, k_ref=k_ref, b_ref=b_ref, c_ref=c_ref,
                  v_ref=v_ref, g_ref=g_ref, cm_ref=cm_ref, a_ref=a_ref,
                  st_refs=(st0_ref, st1_ref), oh_ref=oh_ref, lb_slots=lb_slots)

    for job in projection_jobs(pa_ref, 0):
        job()

    bufs = (pa_ref, pb_ref)
    per_pair = -(-len(ffn_jobs) // n_pairs)
    for j in range(n_pairs):
        ahead = projection_jobs(bufs[(j + 1) % 2], j + 1) if j + 1 < n_pairs else []
        mine = ffn_jobs[j * per_pair:(j + 1) * per_pair]
        diag_fillers = mine[:1] + ahead[:2] + mine[1:2] + ahead[2:]
        _hgrn2_pair(bufs[j % 2], j, (diag_fillers, mine[2:3], mine[3:]), **common)
    u_ref[...] = un_ref[...]


def _ffn_hgrn2_layer(x2, seq, fgain, f_w_in, f_w_out, gain, w_in, lb_logits, layer, out_gain,
                     w_out, casts=(), *, tt=512):
    t, d = x2.shape
    d_ff = f_w_out.shape[0]
    n_tiles = t // tt
    n_slots = lb_logits.shape[0]
    n_pairs = A_HEADS // 2
    n_jobs = 2 * (tt // A_CHUNK)
    wo_pairs = w_out.reshape(n_pairs, 2 * A_DIM, d)
    lbl = lb_logits.reshape(n_slots, A_HEADS, A_DIM).transpose(1, 0, 2)
    side = _side_casts(casts, n_tiles + 1, lambda s: s)
    tril = jnp.asarray(np.tril(np.ones((A_CHUNK, A_CHUNK), np.float32)), BF16)
    idx = np.arange(LANES)
    tile = jnp.asarray((idx[:, None] == idx[None, :] % A_DIAG).astype(np.float32), BF16)
    head_tile = lambda dtype: pltpu.VMEM((2, tt, A_DIM), dtype)
    out, *cast = pl.pallas_call(
        functools.partial(_ffn_hgrn2_kernel, lb_slots=layer + 1, tiles_per_seq=seq // tt,
                          n_casts=len(casts)),
        grid=(n_tiles + 1,),
        in_specs=[
            pl.BlockSpec((tt, d), lambda s: (jnp.minimum(s, n_tiles - 1), 0)),
            _const_spec((1, d)),
            _const_spec((d, 2 * d_ff)),
            _const_spec((d_ff, d)),
            _const_spec((1, d)),
            _const_spec(w_in.shape),
            _const_spec((A_HEADS, n_slots, A_DIM)),
            _const_spec((1, A_DIM)),
            _const_spec((n_pairs, 2 * A_DIM, d)),
            _const_spec((A_CHUNK, A_CHUNK)),
            _const_spec((LANES, LANES)),
        ] + side.in_specs,
        out_specs=[pl.BlockSpec((tt, d), lambda s: (jnp.maximum(s - 1, 0), 0))] + side.out_specs,
        out_shape=[jax.ShapeDtypeStruct((t, d), F32)] + side.out_shapes,
        scratch_shapes=[
            pltpu.VMEM((tt, d), F32),
            pltpu.VMEM((tt, d), F32),
            pltpu.VMEM((tt, d), BF16),
            pltpu.VMEM((tt, d_ff), BF16),
            pltpu.VMEM((tt, d), BF16),
            pltpu.VMEM((tt, 8 * A_DIM), F32),
            pltpu.VMEM((tt, 8 * A_DIM), F32),
            head_tile(F32), head_tile(F32), head_tile(F32),
            head_tile(F32), head_tile(F32), head_tile(F32),
            pltpu.VMEM((n_jobs * A_CHUNK, LANES), BF16),
            pltpu.VMEM((n_jobs, A_CHUNK, LANES), BF16),
            pltpu.VMEM((n_pairs, A_DIM, A_DIM), F32),
            pltpu.VMEM((n_pairs, A_DIM, A_DIM), F32),
            pltpu.VMEM((tt, 2 * A_DIM), BF16),
        ],
        compiler_params=pltpu.CompilerParams(
            dimension_semantics=("arbitrary",), vmem_limit_bytes=VMEM_LIMIT_FUSED),
        name="swiglu_hgrn2_layer",
    )(x2, fgain.reshape(1, d), f_w_in, f_w_out, gain.reshape(1, d), w_in, lbl,
      out_gain.reshape(1, A_DIM), wo_pairs, tril, tile, *side.args)
    return out, cast


def _proj_kernel(x_ref, gain_ref, w_ref, hg_ref, *refs, n_normed, scale, n_out, n_casts):
    cast_in, outs, (hs_ref,) = _split_refs(refs, n_casts, n_out)
    out_refs, cast_out = outs[:n_out], outs[n_out:]
    _run_casts(cast_in, cast_out)
    tm = x_ref.shape[1]
    width = out_refs[0].shape[-1]
    hf = _rms_rows(x_ref[0], gain_ref[...])
    n_slabs = hs_ref.shape[0]
    for s in range(n_slabs):
        hs_ref[s] = hf[:, s * LANES:(s + 1) * LANES]
    lhs = {}
    for dil in sorted(set(B_DILATIONS)):
        if dil == 1:
            lhs[dil] = hf.astype(BF16)
            continue
        slabs = [jnp.concatenate([hs_ref[s, pl.ds(r, tm // dil, stride=dil), :]
                                  for r in range(dil)], axis=0).astype(BF16)
                 for s in range(n_slabs)]
        lhs[dil] = jnp.concatenate(slabs, axis=1)
    for j, o_ref in enumerate(out_refs):
        kind, g = divmod(j, B_GROUPS)
        dil = B_DILATIONS[g]
        for c0 in range(0, width, MXU_COLS):
            col = j * width + c0
            t2 = jnp.dot(lhs[dil], w_ref[:, col:col + MXU_COLS], preferred_element_type=F32)
            for c1 in range(c0, c0 + MXU_COLS, LANES):
                t = t2[:, c1 - c0:c1 - c0 + LANES]
                if kind < n_normed:
                    t = _head_rms_pair(t, hg_ref[g:g + 1, c1:c1 + LANES]) * scale
                o_ref[0, :, :, c1:c1 + LANES] = t.astype(BF16).reshape(dil, tm // dil, LANES)


def _project(x, gain, w, head_gain, casts=(), *, n_kinds, n_normed, scale, tm=512):
    bsz, seq, d = x.shape
    n_out = n_kinds * B_GROUPS
    width = w.shape[1] // n_out
    n_tiles = seq // tm
    hg = jnp.tile(head_gain, (1, B_HEADS))
    dils = [B_DILATIONS[j % B_GROUPS] for j in range(n_out)]
    side = _side_casts(casts, bsz * n_tiles, lambda b, i: b * n_tiles + i)
    outs = pl.pallas_call(
        functools.partial(_proj_kernel, n_normed=n_normed, scale=scale, n_out=n_out,
                          n_casts=len(casts)),
        grid=(bsz, n_tiles),
        in_specs=[
            pl.BlockSpec((1, tm, d), lambda b, i: (b, i, 0)),
            _const_spec((1, d)),
            _const_spec((d, n_out * width)),
            _const_spec((B_GROUPS, width)),
        ] + side.in_specs,
        out_specs=[pl.BlockSpec((1, dil, tm // dil, width), lambda b, i: (b, 0, i, 0))
                   for dil in dils] + side.out_specs,
        out_shape=[jax.ShapeDtypeStruct((bsz, dil, seq // dil, width), BF16) for dil in dils]
                  + side.out_shapes,
        scratch_shapes=[pltpu.VMEM((d // LANES, tm, LANES), F32)],
        compiler_params=pltpu.CompilerParams(
            dimension_semantics=("arbitrary", "arbitrary"), vmem_limit_bytes=VMEM_LIMIT),
        name="norm_projection",
    )(x, gain.reshape(1, d), w, hg, *side.args)
    return outs[:n_out], outs[n_out:]


def _t5_bucket(dist):
    dist = np.asarray(dist, np.int32)
    max_exact = NUM_BUCKETS // 2
    large = max_exact + (np.log(np.maximum(dist, 1) / max_exact)
                         / math.log(MAX_DISTANCE / max_exact) * (NUM_BUCKETS - max_exact)).astype(np.int32)
    large = np.minimum(large, NUM_BUCKETS - 1)
    return np.where(dist < max_exact, dist, large).astype(np.int32)


def _band_bias(rel_bias, group):
    dil = B_DILATIONS[group]
    wd = B_WINDOWS[group] // dil
    i = np.arange(B_BLOCK)[:, None]
    j = np.arange(2 * B_BLOCK)[None, :]
    dist = i + wd - j
    band = (dist >= 0) & (dist <= wd)
    bucket = _t5_bucket(np.clip(dist, 0, wd) * dil)
    ids = jnp.asarray(np.where(band, bucket, -1), jnp.int32)
    onehot = (ids[None] == jnp.arange(NUM_BUCKETS, dtype=jnp.int32)[:, None, None]).astype(F32)
    table = rel_bias[:, group * B_HEADS:(group + 1) * B_HEADS].astype(F32)
    bias = jnp.einsum("bh,bij->hij", table, onehot,
                      precision=lax.Precision.HIGHEST) * LOG2E
    pair_shape = (B_HEADS // 2, 2 * B_BLOCK, 2 * B_BLOCK)
    tables = []
    for keep in (band & (j >= B_BLOCK), band):
        tables.append((bias + jnp.asarray(np.where(keep, 0.0, NEG), F32)[None]).reshape(pair_shape))
    return jnp.stack(tables)


def _stat_lane(head):
    return (head % 2) * (LANES // 2) + head // 2


def _attn_kernel(q_ref, kp_ref, kc_ref, vp_ref, vc_ref, bias_ref, o_ref, m_ref, l_ref):
    nq = B_BLOCK
    half = LANES // 2
    n_tiles = q_ref.shape[2] // nq
    n_pairs = B_HEADS // 2
    first_table = jnp.where(pl.program_id(2) == 0, 0, 1)
    lo_q = _half_lane_mask((nq, LANES))
    lo_k = _half_lane_mask((2 * nq, LANES))
    lane = lax.broadcasted_iota(jnp.int32, (nq, LANES), 1)
    stat_lane = jnp.bitwise_and(lane, half - 1)
    ones_lo = jnp.where(lo_k, 1.0, 0.0).astype(BF16)
    ones_hi = jnp.where(lo_k, 0.0, 1.0).astype(BF16)

    def keys_values(prev_ref, cur_ref, t, sl):
        if t == 0:
            return jnp.concatenate([prev_ref[0, 0, :, sl], cur_ref[0, 0, :nq, sl]], axis=0)
        return cur_ref[0, 0, (t - 1) * nq:(t + 1) * nq, sl]

    def scores(job):
        t, pair = divmod(job, n_pairs)
        sl = slice(pair * LANES, (pair + 1) * LANES)
        qp = q_ref[0, 0, t * nq:(t + 1) * nq, sl]
        kk = keys_values(kp_ref, kc_ref, t, sl)
        zq = jnp.zeros_like(qp)
        qq = jnp.concatenate([jnp.where(lo_q, qp, zq), jnp.where(lo_q, zq, qp)], axis=0)
        table = first_table if t == 0 else 1
        return lax.dot_general(qq, kk, NT_DIMS, preferred_element_type=F32) + bias_ref[table, pair]

    s_next = scores(0)
    for job in range(n_tiles * n_pairs):
        t, pair = divmod(job, n_pairs)
        sl = slice(pair * LANES, (pair + 1) * LANES)
        rows = slice(t * nq, (t + 1) * nq)
        if pair == 0:
            m_tile = jnp.zeros((nq, LANES), F32)
            l_tile = jnp.ones((nq, LANES), F32)
        s = s_next
        if job + 1 < n_tiles * n_pairs:
            s_next = scores(job + 1)
        vv = keys_values(vp_ref, vc_ref, t, sl)
        zv = jnp.zeros_like(vv)
        m = jnp.max(s, axis=-1, keepdims=True)
        p = jnp.exp2(s - m).astype(BF16)
        pk = jnp.concatenate([p[:nq], p[nq:]], axis=1)
        rhs = jnp.concatenate(
            [jnp.concatenate([jnp.where(lo_k, vv, zv), ones_lo], axis=1),
             jnp.concatenate([jnp.where(lo_k, zv, vv), ones_hi], axis=1)], axis=0)
        ov = jnp.dot(pk, rhs, preferred_element_type=F32)
        o_ref[0, 0, rows, sl] = ov[:, :LANES].astype(BF16)
        here = stat_lane == pair
        m_tile = jnp.where(here, jnp.where(lo_q, m[:nq], m[nq:]), m_tile)
        l_tile = jnp.where(here, ov[:, LANES:], l_tile)
        if pair == n_pairs - 1:
            m_ref[0, 0, rows, :] = m_tile
            l_ref[0, 0, rows, :] = l_tile


def _attn_group(q, k, v, rel_bias, group, *, tq=1024):
    bsz, dil, n_sub, width = q.shape
    tq = min(tq, n_sub)
    per_block = tq // B_BLOCK
    cur = lambda w: pl.BlockSpec((1, 1, tq, w), lambda b, r, i: (b, r, i, 0))
    prev = lambda w: pl.BlockSpec((1, 1, B_BLOCK, w),
                                  lambda b, r, i: (b, r, jnp.maximum(i * per_block - 1, 0), 0))
    stats = jax.ShapeDtypeStruct((bsz, dil, n_sub, LANES), F32)
    return pl.pallas_call(
        _attn_kernel,
        grid=(bsz, dil, n_sub // tq),
        in_specs=[cur(width), prev(width), cur(width), prev(width), cur(width),
                  _const_spec((2, B_HEADS // 2, 2 * B_BLOCK, 2 * B_BLOCK))],
        out_specs=[cur(width), cur(LANES), cur(LANES)],
        out_shape=[jax.ShapeDtypeStruct((bsz, dil, n_sub, width), BF16), stats, stats],
        compiler_params=pltpu.CompilerParams(
            dimension_semantics=("arbitrary", "arbitrary", "arbitrary"),
            vmem_limit_bytes=VMEM_LIMIT),
        name=f"dilated_attention_g{group}",
    )(q, k, k, v, v, _band_bias(rel_bias, group))


def _merge_kernel(x_ref, o0_ref, o1_ref, o2_ref, m0_ref, m1_ref, m2_ref, l0_ref, l1_ref, l2_ref,
                  ex_ref, wo_ref, out_ref, slab_ref, lslab_ref):
    tm = x_ref.shape[1]
    n_slabs = slab_ref.shape[0]

    def stat_rows(l_ref):
        dil = l_ref.shape[1]
        if dil == 1:
            return l_ref[0, 0]
        for r in range(dil):
            lslab_ref[pl.ds(r, tm // dil, stride=dil), :] = l_ref[0, r]
        return lslab_ref[...]

    def out_rows(o_ref):
        dil = o_ref.shape[1]
        if dil == 1:
            return o_ref[0, 0].astype(F32)
        for s in range(n_slabs):
            for r in range(dil):
                slab_ref[s, pl.ds(r, tm // dil, stride=dil), :] = (
                    o_ref[0, r, :, s * LANES:(s + 1) * LANES].astype(F32))
        return jnp.concatenate([slab_ref[s] for s in range(n_slabs)], axis=1)

    ms = [stat_rows(m_ref) for m_ref in (m0_ref, m1_ref, m2_ref)]
    ls = [stat_rows(l_ref) for l_ref in (l0_ref, l1_ref, l2_ref)]
    top = jnp.maximum(jnp.maximum(ms[0], ms[1]), ms[2])
    es = [jnp.exp2(m - top) for m in ms]
    inv = 1.0 / (es[0] * ls[0] + es[1] * ls[1] + es[2] * ls[2])
    acc = None
    for e, o_ref in zip(es, (o0_ref, o1_ref, o2_ref)):
        w = jnp.dot((e * inv).astype(BF16), ex_ref[...], preferred_element_type=F32)
        term = w * out_rows(o_ref)
        acc = term if acc is None else acc + term
    out_ref[0] = x_ref[0] + jnp.dot(acc.astype(BF16), wo_ref[...], preferred_element_type=F32)


def _merge_out(x, os, ms, ls, w_o, *, tm=512):
    bsz, seq, d = x.shape
    width = os[0].shape[-1]
    expand = np.zeros((LANES, width), np.float32)
    for h in range(B_HEADS):
        expand[_stat_lane(h), h * B_HEAD_DIM:(h + 1) * B_HEAD_DIM] = 1.0
    row = lambda w: pl.BlockSpec((1, tm, w), lambda b, i: (b, i, 0))
    grouped = lambda a: pl.BlockSpec((1, a.shape[1], tm // a.shape[1], a.shape[3]),
                                     lambda b, i: (b, 0, i, 0))
    return pl.pallas_call(
        _merge_kernel,
        grid=(bsz, seq // tm),
        in_specs=[row(d)] + [grouped(a) for a in (*os, *ms, *ls)]
                 + [_const_spec((LANES, width)), _const_spec((width, d))],
        out_specs=row(d),
        out_shape=jax.ShapeDtypeStruct((bsz, seq, d), F32),
        scratch_shapes=[pltpu.VMEM((width // LANES, tm, LANES), F32),
                        pltpu.VMEM((tm, LANES), F32)],
        compiler_params=pltpu.CompilerParams(
            dimension_semantics=("arbitrary", "arbitrary"), vmem_limit_bytes=VMEM_LIMIT),
        name="merge_out_projection",
    )(x, *os, *ms, *ls, jnp.asarray(expand, BF16), w_o)


def kernel(x, norm_gain, ffn_w_in, ffn_w_out, a_w_in, a_lb_logits, a_out_gain, a_w_out,
           kv_norm, w_kv, k_gain, b_w_q, b_q_gain, b_w_o, rel_bias):
    bsz, seq, d = x.shape
    depth = norm_gain.shape[0]
    n_a = a_w_in.shape[0]
    t = bsz * seq

    ffn_weights = lambda layer, j: [(ffn_w_in, (layer, j)), (ffn_w_out, (layer, j))]
    stages = []
    for layer in range(depth):
        if layer == n_a:
            stages.append(("kv", layer, [(w_kv, ())]))
        if layer < n_a:
            stages.append(("ffn_hgrn2", layer,
                           ffn_weights(layer, 0) + [(a_w_in, (layer,)), (a_w_out, (layer,))]))
        else:
            stages.append(("ffn", (layer, 0), ffn_weights(layer, 0)))
            stages.append(("attention", layer, [(b_w_q, (layer - n_a,)), (b_w_o, (layer - n_a,))]))
        stages.append(("ffn", (layer, 2), ffn_weights(layer, 1)))
    carriers = ("kv", "ffn_hgrn2", "ffn", "attention")

    ready = [[arr[lead].astype(BF16) for arr, lead in stages[0][2]]]
    ks = vs = None
    for i, (kind, where, _) in enumerate(stages):
        own = ready.pop(0)
        ahead = []
        if kind in carriers:
            for later in stages[i + 1:]:
                ahead.append(later[2])
                if later[0] in carriers:
                    break
        casts = [w for ws in ahead for w in ws]
        if kind == "kv":
            (ks_vs, cast) = _project(x, kv_norm, own[0], k_gain, casts,
                                     n_kinds=2, n_normed=1, scale=1.0)
            ks, vs = ks_vs[:B_GROUPS], ks_vs[B_GROUPS:]
        elif kind == "ffn_hgrn2":
            x, cast = _ffn_hgrn2_layer(x.reshape(t, d), seq, norm_gain[where, 0], own[0], own[1],
                                       norm_gain[where, 1], own[2], a_lb_logits, where,
                                       a_out_gain[where], own[3], casts)
            x = x.reshape(bsz, seq, d)
        elif kind == "ffn":
            layer, slot = where
            x, cast = _ffn(x.reshape(t, d), norm_gain[layer, slot], own[0], own[1], casts)
            x = x.reshape(bsz, seq, d)
        else:
            j = where - n_a
            qs, cast = _project(x, norm_gain[where, 1], own[0], b_q_gain[j], casts,
                                n_kinds=1, n_normed=1, scale=B_HEAD_DIM ** -0.5 * LOG2E)
            parts = [_attn_group(qs[g], ks[g], vs[g], rel_bias, g) for g in range(B_GROUPS)]
            os, ms, ls = zip(*parts)
            x = _merge_out(x, os, ms, ls, own[1])
        cast = list(cast)
        for ws in ahead:
            ready.append([cast.pop(0) for _ in ws])
    return x
```

```python
import functools
import math
from typing import NamedTuple

import jax
import jax.numpy as jnp
import numpy as np
from jax import lax
from jax.experimental import pallas as pl
from jax.experimental.pallas import tpu as pltpu

F32 = jnp.float32
BF16 = jnp.bfloat16

EPS = 1e-6
LOG2E = math.log2(math.e)
F32_TINY = float(np.finfo(np.float32).tiny)
LANES = 128
SUBLANES = 8
MXU_COLS = 256
VMEM_LIMIT = 56 * 1024 * 1024
VMEM_LIMIT_FUSED = 60 * 1024 * 1024

A_HEADS = 8
A_DIM = 128
A_CHUNK = 128
A_DIAG = SUBLANES
A_LEVELS = (16, 32, 64, 128)

B_WINDOWS = (128, 512, 2048)
B_DILATIONS = (1, 4, 16)
B_GROUPS = 3
B_HEADS = 16
B_HEAD_DIM = 64
B_BLOCK = 128
NUM_BUCKETS = 32
MAX_DISTANCE = 2048
NEG = -1e30

NT_DIMS = (((1,), (1,)), ((), ()))


def _const_spec(shape):
    nd = len(shape)
    return pl.BlockSpec(shape, lambda *_: (0,) * nd, pipeline_mode=pl.Buffered(1))


def _sigmoid(x):
    return 1.0 / (1.0 + jnp.exp(-x))


def _rms_rows(x, gain):
    ms = jnp.mean(x * x, axis=-1, keepdims=True)
    return x * lax.rsqrt(ms + EPS) * gain


def _half_lane_mask(shape):
    return lax.broadcasted_iota(jnp.int32, shape, len(shape) - 1) < (LANES // 2)


def _head_rms_pair(t, gain_row):
    lo = _half_lane_mask(t.shape)
    sq = t * t
    s_lo = jnp.sum(jnp.where(lo, sq, 0.0), axis=-1, keepdims=True)
    s_hi = jnp.sum(jnp.where(lo, 0.0, sq), axis=-1, keepdims=True)
    ms = jnp.where(lo, s_lo, s_hi) * (1.0 / B_HEAD_DIM)
    return t * lax.rsqrt(ms + EPS) * gain_row


class _SideCasts(NamedTuple):
    in_specs: list
    out_specs: list
    out_shapes: list
    args: list


def _cast_block_rows(rows, steps):
    tile = 2 * SUBLANES
    br = tile
    while rows % br or rows // br > steps:
        br += tile
    return br


def _side_casts(casts, n_steps, step_of):
    side = _SideCasts([], [], [], [])
    for arr, lead in casts:
        rows, cols = arr.shape[-2:]
        br = _cast_block_rows(rows, n_steps)
        last = rows // br - 1

        def block(*g, last=last):
            return jnp.minimum(step_of(*g), last)

        side.in_specs.append(pl.BlockSpec((None,) * len(lead) + (br, cols),
                                          lambda *g, lead=lead, block=block: (*lead, block(*g), 0)))
        side.out_specs.append(pl.BlockSpec((br, cols), lambda *g, block=block: (block(*g), 0)))
        side.out_shapes.append(jax.ShapeDtypeStruct((rows, cols), BF16))
        side.args.append(arr)
    return side


def _split_refs(refs, n_casts, n_out):
    n_mid = n_out + n_casts
    return refs[:n_casts], refs[n_casts:n_casts + n_mid], refs[n_casts + n_mid:]


def _run_casts(cast_in, cast_out):
    for src, dst in zip(cast_in, cast_out):
        dst[...] = src[...].astype(BF16)


def _ffn_jobs(x_ref, gain_ref, wi_ref, wo_ref, h_ref, act_ref, dst_ref):
    d_ff, d = wo_ref.shape

    def norm():
        h_ref[...] = _rms_rows(x_ref[...], gain_ref[...]).astype(BF16)

    def hidden(c0):
        def run():
            sl = slice(c0, c0 + MXU_COLS)
            h = h_ref[...]
            g = jnp.dot(h, wi_ref[:, sl], preferred_element_type=F32)
            u = jnp.dot(h, wi_ref[:, d_ff + c0:d_ff + c0 + MXU_COLS], preferred_element_type=F32)
            act_ref[:, sl] = (g * _sigmoid(g) * u).astype(BF16)
        return run

    def output(c0):
        def run():
            sl = slice(c0, c0 + MXU_COLS)
            y = jnp.dot(act_ref[...], wo_ref[:, sl], preferred_element_type=F32)
            dst_ref[:, sl] = x_ref[:, sl] + 0.5 * y
        return run

    return ([norm] + [hidden(c0) for c0 in range(0, d_ff, MXU_COLS)]
            + [output(c0) for c0 in range(0, d, MXU_COLS)])


def _ffn_kernel(x_ref, gain_ref, wi_ref, wo_ref, *refs, n_casts):
    cast_in, (o_ref, *cast_out), (h_ref, act_ref) = _split_refs(refs, n_casts, 1)
    _run_casts(cast_in, cast_out)
    for job in _ffn_jobs(x_ref, gain_ref, wi_ref, wo_ref, h_ref, act_ref, o_ref):
        job()


def _ffn(x2, gain, w_in, w_out, casts=(), *, tm=1024):
    t, d = x2.shape
    d_ff = w_out.shape[0]
    side = _side_casts(casts, t // tm, lambda i: i)
    out, *cast = pl.pallas_call(
        functools.partial(_ffn_kernel, n_casts=len(casts)),
        grid=(t // tm,),
        in_specs=[
            pl.BlockSpec((tm, d), lambda i: (i, 0)),
            _const_spec((1, d)),
            _const_spec((d, 2 * d_ff)),
            _const_spec((d_ff, d)),
        ] + side.in_specs,
        out_specs=[pl.BlockSpec((tm, d), lambda i: (i, 0))] + side.out_specs,
        out_shape=[jax.ShapeDtypeStruct((t, d), F32)] + side.out_shapes,
        scratch_shapes=[pltpu.VMEM((tm, d), BF16), pltpu.VMEM((tm, d_ff), BF16)],
        compiler_params=pltpu.CompilerParams(
            dimension_semantics=("arbitrary",), vmem_limit_bytes=VMEM_LIMIT),
        name="swiglu_half_step",
    )(x2, gain.reshape(1, d), w_in, w_out, *side.args)
    return out, cast


def _hgrn2_gates(p_ref, e, lb, q_ref, k_ref, c_ref, v_ref, g_ref):
    qp, fp, v, gp = (p_ref[:, (2 * i + e) * A_DIM:(2 * i + e + 1) * A_DIM] for i in range(4))
    z = jnp.exp(-jnp.abs(fp))
    r = 1.0 / (1.0 + z)
    pos = fp >= 0
    sig = jnp.where(pos, r, z * r)
    one_minus_sig = jnp.where(pos, z * r, r)
    f = jnp.maximum(lb + (1.0 - lb) * sig, F32_TINY)
    k = (1.0 - lb) * one_minus_sig
    q_ref[e] = qp * _sigmoid(qp)
    k_ref[e] = k
    c_ref[e] = jnp.log(k) * LOG2E
    v_ref[e] = v
    g_ref[e] = gp * _sigmoid(gp)
    return jnp.log(f) * LOG2E


def _hgrn2_diag(q, b2, cs, tri):
    nb = A_CHUNK // A_DIAG
    q3, b3, c3 = (t.reshape(nb, A_DIAG, LANES) for t in (q, b2, cs))
    lane3 = lax.broadcasted_iota(jnp.int32, (nb, A_DIAG, LANES), 2)
    ch = jnp.zeros((nb, A_DIAG, LANES), F32)
    for s in range(A_DIAG):
        e = jnp.exp2(b3 - c3[:, s:s + 1])
        col = jnp.sum(q3 * e, axis=-1, keepdims=True)
        ch = jnp.where(lane3 == s, col, ch)
    return jnp.where(tri >= 0, ch.reshape(A_CHUNK, LANES), 0.0).astype(BF16)


def _hgrn2_levels(a, q, k, b2, xr):
    c = A_CHUNK
    for level in A_LEVELS:
        half = level // 2
        shape4 = (c // level, 2, half, LANES)
        q4, k4, b4 = (t.reshape(shape4) for t in (q, k, b2))
        b_mid = b4[:, 0, half - 1:half]
        zero = jnp.zeros((c // level, half, LANES), F32)
        qu = q4[:, 1] * jnp.exp2(b4[:, 1] - b_mid)
        kl = k4[:, 0] * jnp.exp2(b_mid - b4[:, 0])
        ql = jnp.concatenate([zero, qu], axis=1).reshape(c, LANES).astype(BF16)
        kl = jnp.concatenate([kl, zero], axis=1).reshape(c, LANES).astype(BF16)
        al = lax.dot_general(ql, kl, NT_DIMS, preferred_element_type=F32)
        a = jnp.where(xr < half, a, al)
    return a


class _PairTiles(NamedTuple):
    q: object
    k: object
    b: object
    c: object
    v: object
    g: object


def _chunk_rows(c):
    return slice(c * A_CHUNK, (c + 1) * A_CHUNK)


def _pair_gate_jobs(p_ref, j, tiles, lbl_ref, tril, *, lb_slots):
    n_ch = p_ref.shape[0] // A_CHUNK
    hi_parts, lo_parts = [], []

    def gates(e):
        def run():
            lg = lbl_ref[2 * j + e]
            ex = jnp.exp(lg - jnp.max(lg, axis=0, keepdims=True))
            lb = (jnp.sum(ex[:lb_slots], axis=0, keepdims=True)
                  / jnp.sum(ex, axis=0, keepdims=True))
            lf2 = _hgrn2_gates(p_ref, e, lb, tiles.q, tiles.k, tiles.c, tiles.v, tiles.g)
            hi = lf2.astype(BF16)
            lo = (lf2 - hi.astype(F32)).astype(BF16)
            hi_parts.extend(hi[_chunk_rows(c)] for c in range(n_ch))
            lo_parts.extend(lo[_chunk_rows(c)] for c in range(n_ch))
        return run

    def decays():
        b_all = (jnp.dot(tril, jnp.concatenate(hi_parts, axis=1), preferred_element_type=F32)
                 + jnp.dot(tril, jnp.concatenate(lo_parts, axis=1), preferred_element_type=F32))
        for i, (e, c) in enumerate((e, c) for e in range(2) for c in range(n_ch)):
            b2 = b_all[:, i * LANES:(i + 1) * LANES]
            tiles.b[e, _chunk_rows(c), :] = b2
            tiles.c[e, _chunk_rows(c), :] = b2 - tiles.c[e, _chunk_rows(c), :]

    return [gates(0), gates(1), decays]


def _run_one(pending, when=True):
    if pending and when:
        pending.pop(0)()


def _pair_blocks(tiles, fillers, tile, xr, tri, cm_ref, a_ref):
    diag_fillers, level_fillers = (list(f) for f in fillers)
    n_ch = tiles.q.shape[1] // A_CHUNK
    jobs = [(e, c) for e in range(2) for c in range(n_ch)]
    for i, (e, c) in enumerate(jobs):
        rows = _chunk_rows(c)
        cm_ref[i * A_CHUNK:(i + 1) * A_CHUNK, :] = _hgrn2_diag(
            tiles.q[e, rows, :], tiles.b[e, rows, :], tiles.c[e, rows, :], tri)
        _run_one(diag_fillers)
    for job in diag_fillers:
        job()
    ad_all = jnp.dot(cm_ref[...], tile, preferred_element_type=F32)
    for i, (e, c) in enumerate(jobs):
        rows = _chunk_rows(c)
        a = _hgrn2_levels(ad_all[i * A_CHUNK:(i + 1) * A_CHUNK], tiles.q[e, rows, :],
                          tiles.k[e, rows, :], tiles.b[e, rows, :], xr)
        a_ref[i] = a.astype(BF16)
        _run_one(level_fillers, i % 2 == 1)
    for job in level_fillers:
        job()


def _pair_recurrence(j, tiles, between, og_ref, wo_ref, a_ref, st_refs, oh_ref, o_ref):
    between = list(between)
    n_ch = tiles.q.shape[1] // A_CHUNK
    for c in range(n_ch):
        rows = _chunk_rows(c)
        for e in range(2):
            st_ref = st_refs[e]
            q, k, b2, v = (r[e, rows, :] for r in (tiles.q, tiles.k, tiles.b, tiles.v))
            st = st_ref[j]
            b_last = b2[A_CHUNK - 1:A_CHUNK, :]
            q_in = (q * jnp.exp2(b2)).astype(BF16)
            o = (jnp.dot(a_ref[e * n_ch + c], v.astype(BF16), preferred_element_type=F32)
                 + lax.dot_general(q_in, st.astype(BF16), NT_DIMS, preferred_element_type=F32))
            k_out = (k * jnp.exp2(b_last - b2)).astype(BF16)
            st_ref[j] = st * jnp.exp2(b_last) + jnp.dot(v.T.astype(BF16), k_out,
                                                         preferred_element_type=F32)
            o = _rms_rows(o, og_ref[...]) * tiles.g[e, rows, :]
            oh_ref[rows, e * A_DIM:(e + 1) * A_DIM] = o.astype(BF16)
        _run_one(between)
    for job in between:
        job()
    o_ref[...] += jnp.dot(oh_ref[...], wo_ref[j], preferred_element_type=F32)


def _ffn_hgrn2_kernel(x_ref, fgain_ref, wfi_ref, wfo_ref, gain_ref, w_ref, lbl_ref, og_ref,
                      wo_ref, tril_ref, tile_ref, *refs, lb_slots, tiles_per_seq, n_casts):
    cast_in, (o_ref, *cast_out), scratch = _split_refs(refs, n_casts, 1)
    (u_ref, un_ref, fh_ref, act_ref, h_ref, pa_ref, pb_ref, q_ref, k_ref, b_ref, c_ref, v_ref,
     g_ref, cm_ref, a_ref, st0_ref, st1_ref, oh_ref) = scratch
    _run_casts(cast_in, cast_out)
    step = pl.program_id(0)

    @pl.when(step == 0)
    def _():
        u_ref[...] = jnp.zeros_like(u_ref)

    @pl.when(lax.rem(jnp.maximum(step - 1, 0), tiles_per_seq) == 0)
    def _():
        st0_ref[...] = jnp.zeros_like(st0_ref)
        st1_ref[...] = jnp.zeros_like(st1_ref)

    ffn_jobs = _ffn_jobs(x_ref, fgain_ref, wfi_ref, wfo_ref, fh_ref, act_ref, un_ref)
    x = u_ref[...]
    o_ref[...] = x
    h_ref[...] = _rms_rows(x, gain_ref[...]).astype(BF16)
    tril = tril_ref[...]
    tile = tile_ref[...]
    row = lax.broadcasted_iota(jnp.int32, (A_CHUNK, LANES), 0)
    lane = lax.broadcasted_iota(jnp.int32, (A_CHUNK, LANES), 1)
    xr = jnp.bitwise_xor(row, lane)
    tri = jnp.bitwise_and(row, A_DIAG - 1) - lane
    n_pairs = wo_ref.shape[0]
    part_width = w_ref.shape[1] // 4

    def projection_jobs(dst_ref, pair):
        def job(part):
            def run():
                src = part * part_width + pair * 2 * A_DIM
                dst_ref[:, part * 2 * A_DIM:(part + 1) * 2 * A_DIM] = jnp.dot(
                    h_ref[...], w_ref[:, src:src + 2 * A_DIM], preferred_element_type=F32)
            return run
        return [job(part) for part in range(4)]

    bufs = (pa_ref, pb_ref)
    tiles = _PairTiles(q_ref, k_ref, b_ref, c_ref, v_ref, g_ref)
    for job in projection_jobs(pa_ref, 0):
        job()
    per_pair = -(-len(ffn_jobs) // n_pairs)
    for j in range(n_pairs):
        ahead = projection_jobs(bufs[(j + 1) % 2], j + 1) if j + 1 < n_pairs else []
        mine = ffn_jobs[j * per_pair:(j + 1) * per_pair]
        diag_fillers = mine[:1] + ahead[:2] + mine[1:2] + ahead[2:]
        for job in _pair_gate_jobs(bufs[j % 2], j, tiles, lbl_ref, tril, lb_slots=lb_slots):
            job()
        _pair_blocks(tiles, (diag_fillers, mine[2:3]), tile, xr, tri, cm_ref, a_ref)
        _pair_recurrence(j, tiles, mine[3:], og_ref, wo_ref, a_ref, (st0_ref, st1_ref),
                         oh_ref, o_ref)
    u_ref[...] = un_ref[...]


def _ffn_hgrn2_layer(x2, seq, fgain, f_w_in, f_w_out, gain, w_in, lb_logits, layer, out_gain,
                     w_out, casts=(), *, tt=512):
    t, d = x2.shape
    d_ff = f_w_out.shape[0]
    n_tiles = t // tt
    n_slots = lb_logits.shape[0]
    n_pairs = A_HEADS // 2
    n_jobs = 2 * (tt // A_CHUNK)
    wo_pairs = w_out.reshape(n_pairs, 2 * A_DIM, d)
    lbl = lb_logits.reshape(n_slots, A_HEADS, A_DIM).transpose(1, 0, 2)
    side = _side_casts(casts, n_tiles + 1, lambda s: s)
    tril = jnp.asarray(np.tril(np.ones((A_CHUNK, A_CHUNK), np.float32)), BF16)
    idx = np.arange(LANES)
    tile = jnp.asarray((idx[:, None] == idx[None, :] % A_DIAG).astype(np.float32), BF16)
    head_tile = lambda dtype: pltpu.VMEM((2, tt, A_DIM), dtype)
    out, *cast = pl.pallas_call(
        functools.partial(_ffn_hgrn2_kernel, lb_slots=layer + 1, tiles_per_seq=seq // tt,
                          n_casts=len(casts)),
        grid=(n_tiles + 1,),
        in_specs=[
            pl.BlockSpec((tt, d), lambda s: (jnp.minimum(s, n_tiles - 1), 0)),
            _const_spec((1, d)),
            _const_spec((d, 2 * d_ff)),
            _const_spec((d_ff, d)),
            _const_spec((1, d)),
            _const_spec(w_in.shape),
            _const_spec((A_HEADS, n_slots, A_DIM)),
            _const_spec((1, A_DIM)),
            _const_spec((n_pairs, 2 * A_DIM, d)),
            _const_spec((A_CHUNK, A_CHUNK)),
            _const_spec((LANES, LANES)),
        ] + side.in_specs,
        out_specs=[pl.BlockSpec((tt, d), lambda s: (jnp.maximum(s - 1, 0), 0))] + side.out_specs,
        out_shape=[jax.ShapeDtypeStruct((t, d), F32)] + side.out_shapes,
        scratch_shapes=[
            pltpu.VMEM((tt, d), F32),
            pltpu.VMEM((tt, d), F32),
            pltpu.VMEM((tt, d), BF16),
            pltpu.VMEM((tt, d_ff), BF16),
            pltpu.VMEM((tt, d), BF16),
            pltpu.VMEM((tt, 8 * A_DIM), F32),
            pltpu.VMEM((tt, 8 * A_DIM), F32),
            head_tile(F32), head_tile(F32), head_tile(F32),
            head_tile(F32), head_tile(F32), head_tile(F32),
            pltpu.VMEM((n_jobs * A_CHUNK, LANES), BF16),
            pltpu.VMEM((n_jobs, A_CHUNK, LANES), BF16),
            pltpu.VMEM((n_pairs, A_DIM, A_DIM), F32),
            pltpu.VMEM((n_pairs, A_DIM, A_DIM), F32),
            pltpu.VMEM((tt, 2 * A_DIM), BF16),
        ],
        compiler_params=pltpu.CompilerParams(
            dimension_semantics=("arbitrary",), vmem_limit_bytes=VMEM_LIMIT_FUSED),
        name="swiglu_hgrn2_layer",
    )(x2, fgain.reshape(1, d), f_w_in, f_w_out, gain.reshape(1, d), w_in, lbl,
      out_gain.reshape(1, A_DIM), wo_pairs, tril, tile, *side.args)
    return out, cast


def _proj_kernel(x_ref, gain_ref, w_ref, hg_ref, *refs, n_normed, scale, n_out, n_casts):
    cast_in, outs, (hs_ref,) = _split_refs(refs, n_casts, n_out)
    out_refs, cast_out = outs[:n_out], outs[n_out:]
    _run_casts(cast_in, cast_out)
    tm = x_ref.shape[1]
    width = out_refs[0].shape[-1]
    hf = _rms_rows(x_ref[0], gain_ref[...])
    n_slabs = hs_ref.shape[0]
    for s in range(n_slabs):
        hs_ref[s] = hf[:, s * LANES:(s + 1) * LANES]
    lhs = {}
    for dil in sorted(set(B_DILATIONS)):
        if dil == 1:
            lhs[dil] = hf.astype(BF16)
            continue
        slabs = [jnp.concatenate([hs_ref[s, pl.ds(r, tm // dil, stride=dil), :]
                                  for r in range(dil)], axis=0).astype(BF16)
                 for s in range(n_slabs)]
        lhs[dil] = jnp.concatenate(slabs, axis=1)
    for j, o_ref in enumerate(out_refs):
        kind, g = divmod(j, B_GROUPS)
        dil = B_DILATIONS[g]
        for c0 in range(0, width, MXU_COLS):
            col = j * width + c0
            t2 = jnp.dot(lhs[dil], w_ref[:, col:col + MXU_COLS], preferred_element_type=F32)
            for c1 in range(c0, c0 + MXU_COLS, LANES):
                t = t2[:, c1 - c0:c1 - c0 + LANES]
                if kind < n_normed:
                    t = _head_rms_pair(t, hg_ref[g:g + 1, c1:c1 + LANES]) * scale
                o_ref[0, :, :, c1:c1 + LANES] = t.astype(BF16).reshape(dil, tm // dil, LANES)


def _project(x, gain, w, head_gain, casts=(), *, n_kinds, n_normed, scale, tm=512):
    bsz, seq, d = x.shape
    n_out = n_kinds * B_GROUPS
    width = w.shape[1] // n_out
    n_tiles = seq // tm
    hg = jnp.tile(head_gain, (1, B_HEADS))
    dils = [B_DILATIONS[j % B_GROUPS] for j in range(n_out)]
    side = _side_casts(casts, bsz * n_tiles, lambda b, i: b * n_tiles + i)
    outs = pl.pallas_call(
        functools.partial(_proj_kernel, n_normed=n_normed, scale=scale, n_out=n_out,
                          n_casts=len(casts)),
        grid=(bsz, n_tiles),
        in_specs=[
            pl.BlockSpec((1, tm, d), lambda b, i: (b, i, 0)),
            _const_spec((1, d)),
            _const_spec((d, n_out * width)),
            _const_spec((B_GROUPS, width)),
        ] + side.in_specs,
        out_specs=[pl.BlockSpec((1, dil, tm // dil, width), lambda b, i: (b, 0, i, 0))
                   for dil in dils] + side.out_specs,
        out_shape=[jax.ShapeDtypeStruct((bsz, dil, seq // dil, width), BF16) for dil in dils]
                  + side.out_shapes,
        scratch_shapes=[pltpu.VMEM((d // LANES, tm, LANES), F32)],
        compiler_params=pltpu.CompilerParams(
            dimension_semantics=("arbitrary", "arbitrary"), vmem_limit_bytes=VMEM_LIMIT),
        name="norm_projection",
    )(x, gain.reshape(1, d), w, hg, *side.args)
    return outs[:n_out], outs[n_out:]


def _t5_bucket(dist):
    dist = np.asarray(dist, np.int32)
    max_exact = NUM_BUCKETS // 2
    large = max_exact + (np.log(np.maximum(dist, 1) / max_exact)
                         / math.log(MAX_DISTANCE / max_exact) * (NUM_BUCKETS - max_exact)).astype(np.int32)
    large = np.minimum(large, NUM_BUCKETS - 1)
    return np.where(dist < max_exact, dist, large).astype(np.int32)


def _band_bias(rel_bias, group):
    dil = B_DILATIONS[group]
    wd = B_WINDOWS[group] // dil
    i = np.arange(B_BLOCK)[:, None]
    j = np.arange(2 * B_BLOCK)[None, :]
    dist = i + wd - j
    band = (dist >= 0) & (dist <= wd)
    bucket = _t5_bucket(np.clip(dist, 0, wd) * dil)
    ids = jnp.asarray(np.where(band, bucket, -1), jnp.int32)
    onehot = (ids[None] == jnp.arange(NUM_BUCKETS, dtype=jnp.int32)[:, None, None]).astype(F32)
    table = rel_bias[:, group * B_HEADS:(group + 1) * B_HEADS].astype(F32)
    bias = jnp.einsum("bh,bij->hij", table, onehot,
                      precision=lax.Precision.HIGHEST) * LOG2E
    pair_shape = (B_HEADS // 2, 2 * B_BLOCK, 2 * B_BLOCK)
    tables = []
    for keep in (band & (j >= B_BLOCK), band):
        tables.append((bias + jnp.asarray(np.where(keep, 0.0, NEG), F32)[None]).reshape(pair_shape))
    return jnp.stack(tables)


def _stat_lane(head):
    return (head % 2) * (LANES // 2) + head // 2


def _attn_kernel(q_ref, kp_ref, kc_ref, vp_ref, vc_ref, bias_ref, o_ref, m_ref, l_ref):
    nq = B_BLOCK
    half = LANES // 2
    n_tiles = q_ref.shape[2] // nq
    n_pairs = B_HEADS // 2
    first_table = jnp.where(pl.program_id(2) == 0, 0, 1)
    lo_q = _half_lane_mask((nq, LANES))
    lo_k = _half_lane_mask((2 * nq, LANES))
    lane = lax.broadcasted_iota(jnp.int32, (nq, LANES), 1)
    stat_lane = jnp.bitwise_and(lane, half - 1)
    ones_lo = jnp.where(lo_k, 1.0, 0.0).astype(BF16)
    ones_hi = jnp.where(lo_k, 0.0, 1.0).astype(BF16)

    def keys_values(prev_ref, cur_ref, t, sl):
        if t == 0:
            return jnp.concatenate([prev_ref[0, 0, :, sl], cur_ref[0, 0, :nq, sl]], axis=0)
        return cur_ref[0, 0, (t - 1) * nq:(t + 1) * nq, sl]

    def scores(job):
        t, pair = divmod(job, n_pairs)
        sl = slice(pair * LANES, (pair + 1) * LANES)
        qp = q_ref[0, 0, t * nq:(t + 1) * nq, sl]
        kk = keys_values(kp_ref, kc_ref, t, sl)
        zq = jnp.zeros_like(qp)
        qq = jnp.concatenate([jnp.where(lo_q, qp, zq), jnp.where(lo_q, zq, qp)], axis=0)
        table = first_table if t == 0 else 1
        return lax.dot_general(qq, kk, NT_DIMS, preferred_element_type=F32) + bias_ref[table, pair]

    s_next = scores(0)
    for job in range(n_tiles * n_pairs):
        t, pair = divmod(job, n_pairs)
        sl = slice(pair * LANES, (pair + 1) * LANES)
        rows = slice(t * nq, (t + 1) * nq)
        if pair == 0:
            m_tile = jnp.zeros((nq, LANES), F32)
            l_tile = jnp.ones((nq, LANES), F32)
        s = s_next
        if job + 1 < n_tiles * n_pairs:
            s_next = scores(job + 1)
        vv = keys_values(vp_ref, vc_ref, t, sl)
        zv = jnp.zeros_like(vv)
        m = jnp.max(s, axis=-1, keepdims=True)
        p = jnp.exp2(s - m).astype(BF16)
        pk = jnp.concatenate([p[:nq], p[nq:]], axis=1)
        rhs = jnp.concatenate(
            [jnp.concatenate([jnp.where(lo_k, vv, zv), ones_lo], axis=1),
             jnp.concatenate([jnp.where(lo_k, zv, vv), ones_hi], axis=1)], axis=0)
        ov = jnp.dot(pk, rhs, preferred_element_type=F32)
        o_ref[0, 0, rows, sl] = ov[:, :LANES].astype(BF16)
        here = stat_lane == pair
        m_tile = jnp.where(here, jnp.where(lo_q, m[:nq], m[nq:]), m_tile)
        l_tile = jnp.where(here, ov[:, LANES:], l_tile)
        if pair == n_pairs - 1:
            m_ref[0, 0, rows, :] = m_tile
            l_ref[0, 0, rows, :] = l_tile


def _attn_group(q, k, v, rel_bias, group, *, tq=1024):
    bsz, dil, n_sub, width = q.shape
    tq = min(tq, n_sub)
    per_block = tq // B_BLOCK
    cur = lambda w: pl.BlockSpec((1, 1, tq, w), lambda b, r, i: (b, r, i, 0))
    prev = lambda w: pl.BlockSpec((1, 1, B_BLOCK, w),
                                  lambda b, r, i: (b, r, jnp.maximum(i * per_block - 1, 0), 0))
    stats = jax.ShapeDtypeStruct((bsz, dil, n_sub, LANES), F32)
    return pl.pallas_call(
        _attn_kernel,
        grid=(bsz, dil, n_sub // tq),
        in_specs=[cur(width), prev(width), cur(width), prev(width), cur(width),
                  _const_spec((2, B_HEADS // 2, 2 * B_BLOCK, 2 * B_BLOCK))],
        out_specs=[cur(width), cur(LANES), cur(LANES)],
        out_shape=[jax.ShapeDtypeStruct((bsz, dil, n_sub, width), BF16), stats, stats],
        compiler_params=pltpu.CompilerParams(
            dimension_semantics=("arbitrary", "arbitrary", "arbitrary"),
            vmem_limit_bytes=VMEM_LIMIT),
        name=f"dilated_attention_g{group}",
    )(q, k, k, v, v, _band_bias(rel_bias, group))


def _merge_kernel(x_ref, o0_ref, o1_ref, o2_ref, m0_ref, m1_ref, m2_ref, l0_ref, l1_ref, l2_ref,
                  ex_ref, wo_ref, out_ref, slab_ref, lslab_ref):
    tm = x_ref.shape[1]
    n_slabs = slab_ref.shape[0]

    def stat_rows(l_ref):
        dil = l_ref.shape[1]
        if dil == 1:
            return l_ref[0, 0]
        for r in range(dil):
            lslab_ref[pl.ds(r, tm // dil, stride=dil), :] = l_ref[0, r]
        return lslab_ref[...]

    def out_rows(o_ref):
        dil = o_ref.shape[1]
        if dil == 1:
            return o_ref[0, 0].astype(F32)
        for s in range(n_slabs):
            for r in range(dil):
                slab_ref[s, pl.ds(r, tm // dil, stride=dil), :] = (
                    o_ref[0, r, :, s * LANES:(s + 1) * LANES].astype(F32))
        return jnp.concatenate([slab_ref[s] for s in range(n_slabs)], axis=1)

    ms = [stat_rows(m_ref) for m_ref in (m0_ref, m1_ref, m2_ref)]
    ls = [stat_rows(l_ref) for l_ref in (l0_ref, l1_ref, l2_ref)]
    top = jnp.maximum(jnp.maximum(ms[0], ms[1]), ms[2])
    es = [jnp.exp2(m - top) for m in ms]
    inv = 1.0 / (es[0] * ls[0] + es[1] * ls[1] + es[2] * ls[2])
    acc = None
    for e, o_ref in zip(es, (o0_ref, o1_ref, o2_ref)):
        w = jnp.dot((e * inv).astype(BF16), ex_ref[...], preferred_element_type=F32)
        term = w * out_rows(o_ref)
        acc = term if acc is None else acc + term
    out_ref[0] = x_ref[0] + jnp.dot(acc.astype(BF16), wo_ref[...], preferred_element_type=F32)


def _merge_out(x, os, ms, ls, w_o, *, tm=1024):
    bsz, seq, d = x.shape
    width = os[0].shape[-1]
    expand = np.zeros((LANES, width), np.float32)
    for h in range(B_HEADS):
        expand[_stat_lane(h), h * B_HEAD_DIM:(h + 1) * B_HEAD_DIM] = 1.0
    row = lambda w: pl.BlockSpec((1, tm, w), lambda b, i: (b, i, 0))
    grouped = lambda a: pl.BlockSpec((1, a.shape[1], tm // a.shape[1], a.shape[3]),
                                     lambda b, i: (b, 0, i, 0))
    return pl.pallas_call(
        _merge_kernel,
        grid=(bsz, seq // tm),
        in_specs=[row(d)] + [grouped(a) for a in (*os, *ms, *ls)]
                 + [_const_spec((LANES, width)), _const_spec((width, d))],
        out_specs=row(d),
        out_shape=jax.ShapeDtypeStruct((bsz, seq, d), F32),
        scratch_shapes=[pltpu.VMEM((width // LANES, tm, LANES), F32),
                        pltpu.VMEM((tm, LANES), F32)],
        compiler_params=pltpu.CompilerParams(
            dimension_semantics=("arbitrary", "arbitrary"), vmem_limit_bytes=VMEM_LIMIT),
        name="merge_out_projection",
    )(x, *os, *ms, *ls, jnp.asarray(expand, BF16), w_o)


def kernel(x, norm_gain, ffn_w_in, ffn_w_out, a_w_in, a_lb_logits, a_out_gain, a_w_out,
           kv_norm, w_kv, k_gain, b_w_q, b_q_gain, b_w_o, rel_bias):
    bsz, seq, d = x.shape
    depth = norm_gain.shape[0]
    n_a = a_w_in.shape[0]
    t = bsz * seq

    ffn_weights = lambda layer, j: [(ffn_w_in, (layer, j)), (ffn_w_out, (layer, j))]
    stages = []
    for layer in range(depth):
        if layer == n_a:
            stages.append(("kv", layer, [(w_kv, ())]))
        if layer < n_a:
            stages.append(("ffn_hgrn2", layer,
                           ffn_weights(layer, 0) + [(a_w_in, (layer,)), (a_w_out, (layer,))]))
        else:
            stages.append(("ffn", (layer, 0), ffn_weights(layer, 0)))
            stages.append(("attention", layer, [(b_w_q, (layer - n_a,)), (b_w_o, (layer - n_a,))]))
        stages.append(("ffn", (layer, 2), ffn_weights(layer, 1)))
    carriers = ("kv", "ffn_hgrn2", "ffn", "attention")

    ready = [[arr[lead].astype(BF16) for arr, lead in stages[0][2]]]
    ks = vs = None
    for i, (kind, where, _) in enumerate(stages):
        own = ready.pop(0)
        ahead = []
        if kind in carriers:
            for later in stages[i + 1:]:
                ahead.append(later[2])
                if later[0] in carriers:
                    break
        casts = [w for ws in ahead for w in ws]
        if kind == "kv":
            (ks_vs, cast) = _project(x, kv_norm, own[0], k_gain, casts,
                                     n_kinds=2, n_normed=1, scale=1.0)
            ks, vs = ks_vs[:B_GROUPS], ks_vs[B_GROUPS:]
        elif kind == "ffn_hgrn2":
            x, cast = _ffn_hgrn2_layer(x.reshape(t, d), seq, norm_gain[where, 0], own[0], own[1],
                                       norm_gain[where, 1], own[2], a_lb_logits, where,
                                       a_out_gain[where], own[3], casts)
            x = x.reshape(bsz, seq, d)
        elif kind == "ffn":
            layer, slot = where
            x, cast = _ffn(x.reshape(t, d), norm_gain[layer, slot], own[0], own[1], casts)
            x = x.reshape(bsz, seq, d)
        else:
            j = where - n_a
            qs, cast = _project(x, norm_gain[where, 1], own[0], b_q_gain[j], casts,
                                n_kinds=1, n_normed=1, scale=B_HEAD_DIM ** -0.5 * LOG2E, tm=1024)
            parts = [_attn_group(qs[g], ks[g], vs[g], rel_bias, g) for g in range(B_GROUPS)]
            os, ms, ls = zip(*parts)
            x = _merge_out(x, os, ms, ls, own[1])
        cast = list(cast)
        for ws in ahead:
            ready.append([cast.pop(0) for _ in ws])
    return x
```

```python
import functools
import math
from typing import NamedTuple

import jax
import jax.numpy as jnp
import numpy as np
from jax import lax
from jax.experimental import pallas as pl
from jax.experimental.pallas import tpu as pltpu

F32 = jnp.float32
BF16 = jnp.bfloat16

EPS = 1e-6
LOG2E = math.log2(math.e)
F32_TINY = float(np.finfo(np.float32).tiny)
LANES = 128
SUBLANES = 8
MXU_COLS = 256
VMEM_LIMIT = 56 * 1024 * 1024
VMEM_LIMIT_FUSED = 60 * 1024 * 1024

A_HEADS = 8
A_DIM = 128
A_CHUNK = 128
A_DIAG = SUBLANES
A_LEVELS = (16, 32, 64, 128)

B_WINDOWS = (128, 512, 2048)
B_DILATIONS = (1, 4, 16)
B_GROUPS = 3
B_HEADS = 16
B_HEAD_DIM = 64
B_BLOCK = 128
NUM_BUCKETS = 32
MAX_DISTANCE = 2048
NEG = -1e30

NT_DIMS = (((1,), (1,)), ((), ()))


def _const_spec(shape):
    nd = len(shape)
    return pl.BlockSpec(shape, lambda *_: (0,) * nd, pipeline_mode=pl.Buffered(1))


def _sigmoid(x):
    return 1.0 / (1.0 + jnp.exp(-x))


def _rms_rows(x, gain):
    ms = jnp.mean(x * x, axis=-1, keepdims=True)
    return x * lax.rsqrt(ms + EPS) * gain


def _half_lane_mask(shape):
    return lax.broadcasted_iota(jnp.int32, shape, len(shape) - 1) < (LANES // 2)


def _head_rms_pair(t, gain_row):
    lo = _half_lane_mask(t.shape)
    sq = t * t
    s_lo = jnp.sum(jnp.where(lo, sq, 0.0), axis=-1, keepdims=True)
    s_hi = jnp.sum(jnp.where(lo, 0.0, sq), axis=-1, keepdims=True)
    ms = jnp.where(lo, s_lo, s_hi) * (1.0 / B_HEAD_DIM)
    return t * lax.rsqrt(ms + EPS) * gain_row


class _SideCasts(NamedTuple):
    in_specs: list
    out_specs: list
    out_shapes: list
    args: list


def _cast_block_rows(rows, steps):
    tile = 2 * SUBLANES
    br = tile
    while rows % br or rows // br > steps:
        br += tile
    return br


def _side_casts(casts, n_steps, step_of):
    side = _SideCasts([], [], [], [])
    for arr, lead in casts:
        rows, cols = arr.shape[-2:]
        br = _cast_block_rows(rows, n_steps)
        last = rows // br - 1

        def block(*g, last=last):
            return jnp.minimum(step_of(*g), last)

        side.in_specs.append(pl.BlockSpec((None,) * len(lead) + (br, cols),
                                          lambda *g, lead=lead, block=block: (*lead, block(*g), 0)))
        side.out_specs.append(pl.BlockSpec((br, cols), lambda *g, block=block: (block(*g), 0)))
        side.out_shapes.append(jax.ShapeDtypeStruct((rows, cols), BF16))
        side.args.append(arr)
    return side


def _split_refs(refs, n_casts, n_out):
    n_mid = n_out + n_casts
    return refs[:n_casts], refs[n_casts:n_casts + n_mid], refs[n_casts + n_mid:]


def _run_casts(cast_in, cast_out):
    for src, dst in zip(cast_in, cast_out):
        dst[...] = src[...].astype(BF16)


def _ffn_jobs(x_ref, gain_ref, wi_ref, wo_ref, h_ref, act_ref, dst_ref):
    d_ff, d = wo_ref.shape

    def norm():
        h_ref[...] = _rms_rows(x_ref[...], gain_ref[...]).astype(BF16)

    def hidden(c0):
        def run():
            sl = slice(c0, c0 + MXU_COLS)
            h = h_ref[...]
            g = jnp.dot(h, wi_ref[:, sl], preferred_element_type=F32)
            u = jnp.dot(h, wi_ref[:, d_ff + c0:d_ff + c0 + MXU_COLS], preferred_element_type=F32)
            act_ref[:, sl] = (g * _sigmoid(g) * u).astype(BF16)
        return run

    def output(c0):
        def run():
            sl = slice(c0, c0 + MXU_COLS)
            y = jnp.dot(act_ref[...], wo_ref[:, sl], preferred_element_type=F32)
            dst_ref[:, sl] = x_ref[:, sl] + 0.5 * y
        return run

    return ([norm] + [hidden(c0) for c0 in range(0, d_ff, MXU_COLS)]
            + [output(c0) for c0 in range(0, d, MXU_COLS)])


def _ffn_kernel(x_ref, gain_ref, wi_ref, wo_ref, *refs, n_casts):
    cast_in, (o_ref, *cast_out), (h_ref, act_ref) = _split_refs(refs, n_casts, 1)
    _run_casts(cast_in, cast_out)
    for job in _ffn_jobs(x_ref, gain_ref, wi_ref, wo_ref, h_ref, act_ref, o_ref):
        job()


def _ffn(x2, gain, w_in, w_out, casts=(), *, tm=1024):
    t, d = x2.shape
    d_ff = w_out.shape[0]
    side = _side_casts(casts, t // tm, lambda i: i)
    out, *cast = pl.pallas_call(
        functools.partial(_ffn_kernel, n_casts=len(casts)),
        grid=(t // tm,),
        in_specs=[
            pl.BlockSpec((tm, d), lambda i: (i, 0)),
            _const_spec((1, d)),
            _const_spec((d, 2 * d_ff)),
            _const_spec((d_ff, d)),
        ] + side.in_specs,
        out_specs=[pl.BlockSpec((tm, d), lambda i: (i, 0))] + side.out_specs,
        out_shape=[jax.ShapeDtypeStruct((t, d), F32)] + side.out_shapes,
        scratch_shapes=[pltpu.VMEM((tm, d), BF16), pltpu.VMEM((tm, d_ff), BF16)],
        compiler_params=pltpu.CompilerParams(
            dimension_semantics=("arbitrary",), vmem_limit_bytes=VMEM_LIMIT),
        name="swiglu_half_step",
    )(x2, gain.reshape(1, d), w_in, w_out, *side.args)
    return out, cast


def _hgrn2_gates(p_ref, e, lb, q_ref, k_ref, c_ref, v_ref, g_ref):
    qp, fp, v, gp = (p_ref[:, (2 * i + e) * A_DIM:(2 * i + e + 1) * A_DIM] for i in range(4))
    z = jnp.exp(-jnp.abs(fp))
    r = 1.0 / (1.0 + z)
    pos = fp >= 0
    sig = jnp.where(pos, r, z * r)
    one_minus_sig = jnp.where(pos, z * r, r)
    f = jnp.maximum(lb + (1.0 - lb) * sig, F32_TINY)
    k = (1.0 - lb) * one_minus_sig
    q_ref[e] = qp * _sigmoid(qp)
    k_ref[e] = k
    c_ref[e] = jnp.log(k) * LOG2E
    v_ref[e] = v
    g_ref[e] = gp * _sigmoid(gp)
    return jnp.log(f) * LOG2E


def _hgrn2_diag(q, b2, cs, tri):
    nb = A_CHUNK // A_DIAG
    q3, b3, c3 = (t.reshape(nb, A_DIAG, LANES) for t in (q, b2, cs))
    lane3 = lax.broadcasted_iota(jnp.int32, (nb, A_DIAG, LANES), 2)
    ch = jnp.zeros((nb, A_DIAG, LANES), F32)
    for s in range(A_DIAG):
        e = jnp.exp2(b3 - c3[:, s:s + 1])
        col = jnp.sum(q3 * e, axis=-1, keepdims=True)
        ch = jnp.where(lane3 == s, col, ch)
    ch = jnp.where(tri.reshape(ch.shape) >= 0, ch, 0.0)
    blocks = [ch[blk] if blk == 0 else pltpu.roll(ch[blk], A_DIAG * blk, 1) for blk in range(nb)]
    return jnp.concatenate(blocks, axis=0)


def _hgrn2_levels(a, q, k, b2, xr):
    c = A_CHUNK
    for level in A_LEVELS:
        half = level // 2
        shape4 = (c // level, 2, half, LANES)
        q4, k4, b4 = (t.reshape(shape4) for t in (q, k, b2))
        b_mid = b4[:, 0, half - 1:half]
        zero = jnp.zeros((c // level, half, LANES), F32)
        qu = q4[:, 1] * jnp.exp2(b4[:, 1] - b_mid)
        kl = k4[:, 0] * jnp.exp2(b_mid - b4[:, 0])
        ql = jnp.concatenate([zero, qu], axis=1).reshape(c, LANES).astype(BF16)
        kl = jnp.concatenate([kl, zero], axis=1).reshape(c, LANES).astype(BF16)
        al = lax.dot_general(ql, kl, NT_DIMS, preferred_element_type=F32)
        a = jnp.where(xr < half, a, al)
    return a


class _PairTiles(NamedTuple):
    q: object
    k: object
    b: object
    c: object
    v: object
    g: object


def _chunk_rows(c):
    return slice(c * A_CHUNK, (c + 1) * A_CHUNK)


def _pair_gate_jobs(p_ref, j, tiles, lbl_ref, tril, *, lb_slots):
    n_ch = p_ref.shape[0] // A_CHUNK
    hi_parts, lo_parts = [], []

    def gates(e):
        def run():
            lg = lbl_ref[2 * j + e]
            ex = jnp.exp(lg - jnp.max(lg, axis=0, keepdims=True))
            lb = (jnp.sum(ex[:lb_slots], axis=0, keepdims=True)
                  / jnp.sum(ex, axis=0, keepdims=True))
            lf2 = _hgrn2_gates(p_ref, e, lb, tiles.q, tiles.k, tiles.c, tiles.v, tiles.g)
            hi = lf2.astype(BF16)
            lo = (lf2 - hi.astype(F32)).astype(BF16)
            hi_parts.extend(hi[_chunk_rows(c)] for c in range(n_ch))
            lo_parts.extend(lo[_chunk_rows(c)] for c in range(n_ch))
        return run

    def decays():
        b_all = (jnp.dot(tril, jnp.concatenate(hi_parts, axis=1), preferred_element_type=F32)
                 + jnp.dot(tril, jnp.concatenate(lo_parts, axis=1), preferred_element_type=F32))
        for i, (e, c) in enumerate((e, c) for e in range(2) for c in range(n_ch)):
            b2 = b_all[:, i * LANES:(i + 1) * LANES]
            tiles.b[e, _chunk_rows(c), :] = b2
            tiles.c[e, _chunk_rows(c), :] = b2 - tiles.c[e, _chunk_rows(c), :]

    return [gates(0), gates(1), decays]


def _run_one(pending, when=True):
    if pending and when:
        pending.pop(0)()


def _pair_blocks(tiles, fillers, xr, tri, ad_ref, a_ref):
    diag_fillers, level_fillers = (list(f) for f in fillers)
    n_ch = tiles.q.shape[1] // A_CHUNK
    jobs = [(e, c) for e in range(2) for c in range(n_ch)]
    for i, (e, c) in enumerate(jobs):
        rows = _chunk_rows(c)
        ad_ref[i] = _hgrn2_diag(tiles.q[e, rows, :], tiles.b[e, rows, :], tiles.c[e, rows, :], tri)
        _run_one(diag_fillers)
    for job in diag_fillers:
        job()
    for i, (e, c) in enumerate(jobs):
        rows = _chunk_rows(c)
        a = _hgrn2_levels(ad_ref[i], tiles.q[e, rows, :], tiles.k[e, rows, :],
                          tiles.b[e, rows, :], xr)
        a_ref[i] = a.astype(BF16)
        _run_one(level_fillers, i % 2 == 1)
    for job in level_fillers:
        job()


def _pair_recurrence(j, tiles, between, og_ref, wo_ref, a_ref, st_refs, oh_ref, o_ref):
    between = list(between)
    n_ch = tiles.q.shape[1] // A_CHUNK
    for c in range(n_ch):
        rows = _chunk_rows(c)
        for e in range(2):
            st_ref = st_refs[e]
            q, k, b2, v = (r[e, rows, :] for r in (tiles.q, tiles.k, tiles.b, tiles.v))
            st = st_ref[j]
            b_last = b2[A_CHUNK - 1:A_CHUNK, :]
            q_in = (q * jnp.exp2(b2)).astype(BF16)
            o = (jnp.dot(a_ref[e * n_ch + c], v.astype(BF16), preferred_element_type=F32)
                 + lax.dot_general(q_in, st.astype(BF16), NT_DIMS, preferred_element_type=F32))
            k_out = (k * jnp.exp2(b_last - b2)).astype(BF16)
            st_ref[j] = st * jnp.exp2(b_last) + jnp.dot(v.T.astype(BF16), k_out,
                                                         preferred_element_type=F32)
            o = _rms_rows(o, og_ref[...]) * tiles.g[e, rows, :]
            oh_ref[rows, e * A_DIM:(e + 1) * A_DIM] = o.astype(BF16)
        _run_one(between)
    for job in between:
        job()
    o_ref[...] += jnp.dot(oh_ref[...], wo_ref[j], preferred_element_type=F32)


def _ffn_hgrn2_kernel(x_ref, fgain_ref, wfi_ref, wfo_ref, gain_ref, w_ref, lbl_ref, og_ref,
                      wo_ref, tril_ref, *refs, lb_slots, tiles_per_seq, n_casts):
    cast_in, (o_ref, *cast_out), scratch = _split_refs(refs, n_casts, 1)
    (u_ref, un_ref, fh_ref, act_ref, h_ref, pa_ref, pb_ref, q_ref, k_ref, b_ref, c_ref, v_ref,
     g_ref, ad_ref, a_ref, st0_ref, st1_ref, oh_ref) = scratch
    _run_casts(cast_in, cast_out)
    step = pl.program_id(0)

    @pl.when(step == 0)
    def _():
        u_ref[...] = jnp.zeros_like(u_ref)

    @pl.when(lax.rem(jnp.maximum(step - 1, 0), tiles_per_seq) == 0)
    def _():
        st0_ref[...] = jnp.zeros_like(st0_ref)
        st1_ref[...] = jnp.zeros_like(st1_ref)

    ffn_jobs = _ffn_jobs(x_ref, fgain_ref, wfi_ref, wfo_ref, fh_ref, act_ref, un_ref)
    x = u_ref[...]
    o_ref[...] = x
    h_ref[...] = _rms_rows(x, gain_ref[...]).astype(BF16)
    tril = tril_ref[...]
    row = lax.broadcasted_iota(jnp.int32, (A_CHUNK, LANES), 0)
    lane = lax.broadcasted_iota(jnp.int32, (A_CHUNK, LANES), 1)
    xr = jnp.bitwise_xor(row, lane)
    tri = jnp.bitwise_and(row, A_DIAG - 1) - lane
    n_pairs = wo_ref.shape[0]
    part_width = w_ref.shape[1] // 4

    def projection_jobs(dst_ref, pair):
        def job(part):
            def run():
                src = part * part_width + pair * 2 * A_DIM
                dst_ref[:, part * 2 * A_DIM:(part + 1) * 2 * A_DIM] = jnp.dot(
                    h_ref[...], w_ref[:, src:src + 2 * A_DIM], preferred_element_type=F32)
            return run
        return [job(part) for part in range(4)]

    bufs = (pa_ref, pb_ref)
    tiles = _PairTiles(q_ref, k_ref, b_ref, c_ref, v_ref, g_ref)
    for job in projection_jobs(pa_ref, 0):
        job()
    per_pair = -(-len(ffn_jobs) // n_pairs)
    for j in range(n_pairs):
        ahead = projection_jobs(bufs[(j + 1) % 2], j + 1) if j + 1 < n_pairs else []
        mine = ffn_jobs[j * per_pair:(j + 1) * per_pair]
        diag_fillers = mine[:1] + ahead[:2] + mine[1:2] + ahead[2:]
        for job in _pair_gate_jobs(bufs[j % 2], j, tiles, lbl_ref, tril, lb_slots=lb_slots):
            job()
        _pair_blocks(tiles, (diag_fillers, mine[2:3]), xr, tri, ad_ref, a_ref)
        _pair_recurrence(j, tiles, mine[3:], og_ref, wo_ref, a_ref, (st0_ref, st1_ref),
                         oh_ref, o_ref)
    u_ref[...] = un_ref[...]


def _ffn_hgrn2_layer(x2, seq, fgain, f_w_in, f_w_out, gain, w_in, lb_logits, layer, out_gain,
                     w_out, casts=(), *, tt=512):
    t, d = x2.shape
    d_ff = f_w_out.shape[0]
    n_tiles = t // tt
    n_slots = lb_logits.shape[0]
    n_pairs = A_HEADS // 2
    n_jobs = 2 * (tt // A_CHUNK)
    wo_pairs = w_out.reshape(n_pairs, 2 * A_DIM, d)
    lbl = lb_logits.reshape(n_slots, A_HEADS, A_DIM).transpose(1, 0, 2)
    side = _side_casts(casts, n_tiles + 1, lambda s: s)
    tril = jnp.asarray(np.tril(np.ones((A_CHUNK, A_CHUNK), np.float32)), BF16)
    head_tile = lambda dtype: pltpu.VMEM((2, tt, A_DIM), dtype)
    out, *cast = pl.pallas_call(
        functools.partial(_ffn_hgrn2_kernel, lb_slots=layer + 1, tiles_per_seq=seq // tt,
                          n_casts=len(casts)),
        grid=(n_tiles + 1,),
        in_specs=[
            pl.BlockSpec((tt, d), lambda s: (jnp.minimum(s, n_tiles - 1), 0)),
            _const_spec((1, d)),
            _const_spec((d, 2 * d_ff)),
            _const_spec((d_ff, d)),
            _const_spec((1, d)),
            _const_spec(w_in.shape),
            _const_spec((A_HEADS, n_slots, A_DIM)),
            _const_spec((1, A_DIM)),
            _const_spec((n_pairs, 2 * A_DIM, d)),
            _const_spec((A_CHUNK, A_CHUNK)),
        ] + side.in_specs,
        out_specs=[pl.BlockSpec((tt, d), lambda s: (jnp.maximum(s - 1, 0), 0))] + side.out_specs,
        out_shape=[jax.ShapeDtypeStruct((t, d), F32)] + side.out_shapes,
        scratch_shapes=[
            pltpu.VMEM((tt, d), F32),
            pltpu.VMEM((tt, d), F32),
            pltpu.VMEM((tt, d), BF16),
            pltpu.VMEM((tt, d_ff), BF16),
            pltpu.VMEM((tt, d), BF16),
            pltpu.VMEM((tt, 8 * A_DIM), F32),
            pltpu.VMEM((tt, 8 * A_DIM), F32),
            head_tile(F32), head_tile(F32), head_tile(F32),
            head_tile(F32), head_tile(F32), head_tile(F32),
            pltpu.VMEM((n_jobs, A_CHUNK, LANES), F32),
            pltpu.VMEM((n_jobs, A_CHUNK, LANES), BF16),
            pltpu.VMEM((n_pairs, A_DIM, A_DIM), F32),
            pltpu.VMEM((n_pairs, A_DIM, A_DIM), F32),
            pltpu.VMEM((tt, 2 * A_DIM), BF16),
        ],
        compiler_params=pltpu.CompilerParams(
            dimension_semantics=("arbitrary",), vmem_limit_bytes=VMEM_LIMIT_FUSED),
        name="swiglu_hgrn2_layer",
    )(x2, fgain.reshape(1, d), f_w_in, f_w_out, gain.reshape(1, d), w_in, lbl,
      out_gain.reshape(1, A_DIM), wo_pairs, tril, *side.args)
    return out, cast


def _proj_kernel(x_ref, gain_ref, w_ref, hg_ref, *refs, n_normed, scale, n_out, n_casts):
    cast_in, outs, (hs_ref,) = _split_refs(refs, n_casts, n_out)
    out_refs, cast_out = outs[:n_out], outs[n_out:]
    _run_casts(cast_in, cast_out)
    tm = x_ref.shape[1]
    width = out_refs[0].shape[-1]
    hf = _rms_rows(x_ref[0], gain_ref[...])
    n_slabs = hs_ref.shape[0]
    for s in range(n_slabs):
        hs_ref[s] = hf[:, s * LANES:(s + 1) * LANES]
    lhs = {}
    for dil in sorted(set(B_DILATIONS)):
        if dil == 1:
            lhs[dil] = hf.astype(BF16)
            continue
        slabs = [jnp.concatenate([hs_ref[s, pl.ds(r, tm // dil, stride=dil), :]
                                  for r in range(dil)], axis=0).astype(BF16)
                 for s in range(n_slabs)]
        lhs[dil] = jnp.concatenate(slabs, axis=1)
    for j, o_ref in enumerate(out_refs):
        kind, g = divmod(j, B_GROUPS)
        dil = B_DILATIONS[g]
        for c0 in range(0, width, MXU_COLS):
            col = j * width + c0
            t2 = jnp.dot(lhs[dil], w_ref[:, col:col + MXU_COLS], preferred_element_type=F32)
            for c1 in range(c0, c0 + MXU_COLS, LANES):
                t = t2[:, c1 - c0:c1 - c0 + LANES]
                if kind < n_normed:
                    t = _head_rms_pair(t, hg_ref[g:g + 1, c1:c1 + LANES]) * scale
                o_ref[0, :, :, c1:c1 + LANES] = t.astype(BF16).reshape(dil, tm // dil, LANES)


def _project(x, gain, w, head_gain, casts=(), *, n_kinds, n_normed, scale, tm=512):
    bsz, seq, d = x.shape
    n_out = n_kinds * B_GROUPS
    width = w.shape[1] // n_out
    n_tiles = seq // tm
    hg = jnp.tile(head_gain, (1, B_HEADS))
    dils = [B_DILATIONS[j % B_GROUPS] for j in range(n_out)]
    side = _side_casts(casts, bsz * n_tiles, lambda b, i: b * n_tiles + i)
    outs = pl.pallas_call(
        functools.partial(_proj_kernel, n_normed=n_normed, scale=scale, n_out=n_out,
                          n_casts=len(casts)),
        grid=(bsz, n_tiles),
        in_specs=[
            pl.BlockSpec((1, tm, d), lambda b, i: (b, i, 0)),
            _const_spec((1, d)),
            _const_spec((d, n_out * width)),
            _const_spec((B_GROUPS, width)),
        ] + side.in_specs,
        out_specs=[pl.BlockSpec((1, dil, tm // dil, width), lambda b, i: (b, 0, i, 0))
                   for dil in dils] + side.out_specs,
        out_shape=[jax.ShapeDtypeStruct((bsz, dil, seq // dil, width), BF16) for dil in dils]
                  + side.out_shapes,
        scratch_shapes=[pltpu.VMEM((d // LANES, tm, LANES), F32)],
        compiler_params=pltpu.CompilerParams(
            dimension_semantics=("arbitrary", "arbitrary"), vmem_limit_bytes=VMEM_LIMIT),
        name="norm_projection",
    )(x, gain.reshape(1, d), w, hg, *side.args)
    return outs[:n_out], outs[n_out:]


def _t5_bucket(dist):
    dist = np.asarray(dist, np.int32)
    max_exact = NUM_BUCKETS // 2
    large = max_exact + (np.log(np.maximum(dist, 1) / max_exact)
                         / math.log(MAX_DISTANCE / max_exact) * (NUM_BUCKETS - max_exact)).astype(np.int32)
    large = np.minimum(large, NUM_BUCKETS - 1)
    return np.where(dist < max_exact, dist, large).astype(np.int32)


def _band_bias(rel_bias, group):
    dil = B_DILATIONS[group]
    wd = B_WINDOWS[group] // dil
    i = np.arange(B_BLOCK)[:, None]
    j = np.arange(2 * B_BLOCK)[None, :]
    dist = i + wd - j
    band = (dist >= 0) & (dist <= wd)
    bucket = _t5_bucket(np.clip(dist, 0, wd) * dil)
    ids = jnp.asarray(np.where(band, bucket, -1), jnp.int32)
    onehot = (ids[None] == jnp.arange(NUM_BUCKETS, dtype=jnp.int32)[:, None, None]).astype(F32)
    table = rel_bias[:, group * B_HEADS:(group + 1) * B_HEADS].astype(F32)
    bias = jnp.einsum("bh,bij->hij", table, onehot,
                      precision=lax.Precision.HIGHEST) * LOG2E
    pair_shape = (B_HEADS // 2, 2 * B_BLOCK, 2 * B_BLOCK)
    tables = []
    for keep in (band & (j >= B_BLOCK), band):
        tables.append((bias + jnp.asarray(np.where(keep, 0.0, NEG), F32)[None]).reshape(pair_shape))
    return jnp.stack(tables)


def _stat_lane(head):
    return (head % 2) * (LANES // 2) + head // 2


def _attn_kernel(q_ref, kp_ref, kc_ref, vp_ref, vc_ref, bias_ref, o_ref, m_ref, l_ref):
    nq = B_BLOCK
    half = LANES // 2
    n_tiles = q_ref.shape[2] // nq
    n_pairs = B_HEADS // 2
    first_table = jnp.where(pl.program_id(2) == 0, 0, 1)
    lo_q = _half_lane_mask((nq, LANES))
    lo_k = _half_lane_mask((2 * nq, LANES))
    lane = lax.broadcasted_iota(jnp.int32, (nq, LANES), 1)
    stat_lane = jnp.bitwise_and(lane, half - 1)
    ones_lo = jnp.where(lo_k, 1.0, 0.0).astype(BF16)
    ones_hi = jnp.where(lo_k, 0.0, 1.0).astype(BF16)

    def keys_values(prev_ref, cur_ref, t, sl):
        if t == 0:
            return jnp.concatenate([prev_ref[0, 0, :, sl], cur_ref[0, 0, :nq, sl]], axis=0)
        return cur_ref[0, 0, (t - 1) * nq:(t + 1) * nq, sl]

    def scores(job):
        t, pair = divmod(job, n_pairs)
        sl = slice(pair * LANES, (pair + 1) * LANES)
        qp = q_ref[0, 0, t * nq:(t + 1) * nq, sl]
        kk = keys_values(kp_ref, kc_ref, t, sl)
        zq = jnp.zeros_like(qp)
        qq = jnp.concatenate([jnp.where(lo_q, qp, zq), jnp.where(lo_q, zq, qp)], axis=0)
        table = first_table if t == 0 else 1
        return lax.dot_general(qq, kk, NT_DIMS, preferred_element_type=F32) + bias_ref[table, pair]

    s_next = scores(0)
    for job in range(n_tiles * n_pairs):
        t, pair = divmod(job, n_pairs)
        sl = slice(pair * LANES, (pair + 1) * LANES)
        rows = slice(t * nq, (t + 1) * nq)
        if pair == 0:
            m_tile = jnp.zeros((nq, LANES), F32)
            l_tile = jnp.ones((nq, LANES), F32)
        s = s_next
        if job + 1 < n_tiles * n_pairs:
            s_next = scores(job + 1)
        vv = keys_values(vp_ref, vc_ref, t, sl)
        zv = jnp.zeros_like(vv)
        m = jnp.max(s, axis=-1, keepdims=True)
        p = jnp.exp2(s - m).astype(BF16)
        pk = jnp.concatenate([p[:nq], p[nq:]], axis=1)
        rhs = jnp.concatenate(
            [jnp.concatenate([jnp.where(lo_k, vv, zv), ones_lo], axis=1),
             jnp.concatenate([jnp.where(lo_k, zv, vv), ones_hi], axis=1)], axis=0)
        ov = jnp.dot(pk, rhs, preferred_element_type=F32)
        o_ref[0, 0, rows, sl] = ov[:, :LANES].astype(BF16)
        here = stat_lane == pair
        m_tile = jnp.where(here, jnp.where(lo_q, m[:nq], m[nq:]), m_tile)
        l_tile = jnp.where(here, ov[:, LANES:], l_tile)
        if pair == n_pairs - 1:
            m_ref[0, 0, rows, :] = m_tile
            l_ref[0, 0, rows, :] = l_tile


def _attn_group(q, k, v, rel_bias, group, *, tq=1024):
    bsz, dil, n_sub, width = q.shape
    tq = min(tq, n_sub)
    per_block = tq // B_BLOCK
    cur = lambda w: pl.BlockSpec((1, 1, tq, w), lambda b, r, i: (b, r, i, 0))
    prev = lambda w: pl.BlockSpec((1, 1, B_BLOCK, w),
                                  lambda b, r, i: (b, r, jnp.maximum(i * per_block - 1, 0), 0))
    stats = jax.ShapeDtypeStruct((bsz, dil, n_sub, LANES), F32)
    return pl.pallas_call(
        _attn_kernel,
        grid=(bsz, dil, n_sub // tq),
        in_specs=[cur(width), prev(width), cur(width), prev(width), cur(width),
                  _const_spec((2, B_HEADS // 2, 2 * B_BLOCK, 2 * B_BLOCK))],
        out_specs=[cur(width), cur(LANES), cur(LANES)],
        out_shape=[jax.ShapeDtypeStruct((bsz, dil, n_sub, width), BF16), stats, stats],
        compiler_params=pltpu.CompilerParams(
            dimension_semantics=("arbitrary", "arbitrary", "arbitrary"),
            vmem_limit_bytes=VMEM_LIMIT),
        name=f"dilated_attention_g{group}",
    )(q, k, k, v, v, _band_bias(rel_bias, group))


def _merge_kernel(x_ref, o0_ref, o1_ref, o2_ref, m0_ref, m1_ref, m2_ref, l0_ref, l1_ref, l2_ref,
                  ex_ref, wo_ref, out_ref, slab_ref, lslab_ref):
    tm = x_ref.shape[1]
    n_slabs = slab_ref.shape[0]

    def stat_rows(l_ref):
        dil = l_ref.shape[1]
        if dil == 1:
            return l_ref[0, 0]
        for r in range(dil):
            lslab_ref[pl.ds(r, tm // dil, stride=dil), :] = l_ref[0, r]
        return lslab_ref[...]

    def out_rows(o_ref):
        dil = o_ref.shape[1]
        if dil == 1:
            return o_ref[0, 0].astype(F32)
        for s in range(n_slabs):
            for r in range(dil):
                slab_ref[s, pl.ds(r, tm // dil, stride=dil), :] = (
                    o_ref[0, r, :, s * LANES:(s + 1) * LANES].astype(F32))
        return jnp.concatenate([slab_ref[s] for s in range(n_slabs)], axis=1)

    ms = [stat_rows(m_ref) for m_ref in (m0_ref, m1_ref, m2_ref)]
    ls = [stat_rows(l_ref) for l_ref in (l0_ref, l1_ref, l2_ref)]
    top = jnp.maximum(jnp.maximum(ms[0], ms[1]), ms[2])
    es = [jnp.exp2(m - top) for m in ms]
    inv = 1.0 / (es[0] * ls[0] + es[1] * ls[1] + es[2] * ls[2])
    acc = None
    for e, o_ref in zip(es, (o0_ref, o1_ref, o2_ref)):
        w = jnp.dot((e * inv).astype(BF16), ex_ref[...], preferred_element_type=F32)
        term = w * out_rows(o_ref)
        acc = term if acc is None else acc + term
    out_ref[0] = x_ref[0] + jnp.dot(acc.astype(BF16), wo_ref[...], preferred_element_type=F32)


def _merge_out(x, os, ms, ls, w_o, *, tm=1024):
    bsz, seq, d = x.shape
    width = os[0].shape[-1]
    expand = np.zeros((LANES, width), np.float32)
    for h in range(B_HEADS):
        expand[_stat_lane(h), h * B_HEAD_DIM:(h + 1) * B_HEAD_DIM] = 1.0
    row = lambda w: pl.BlockSpec((1, tm, w), lambda b, i: (b, i, 0))
    grouped = lambda a: pl.BlockSpec((1, a.shape[1], tm // a.shape[1], a.shape[3]),
                                     lambda b, i: (b, 0, i, 0))
    return pl.pallas_call(
        _merge_kernel,
        grid=(bsz, seq // tm),
        in_specs=[row(d)] + [grouped(a) for a in (*os, *ms, *ls)]
                 + [_const_spec((LANES, width)), _const_spec((width, d))],
        out_specs=row(d),
        out_shape=jax.ShapeDtypeStruct((bsz, seq, d), F32),
        scratch_shapes=[pltpu.VMEM((width // LANES, tm, LANES), F32),
                        pltpu.VMEM((tm, LANES), F32)],
        compiler_params=pltpu.CompilerParams(
            dimension_semantics=("arbitrary", "arbitrary"), vmem_limit_bytes=VMEM_LIMIT),
        name="merge_out_projection",
    )(x, *os, *ms, *ls, jnp.asarray(expand, BF16), w_o)


def kernel(x, norm_gain, ffn_w_in, ffn_w_out, a_w_in, a_lb_logits, a_out_gain, a_w_out,
           kv_norm, w_kv, k_gain, b_w_q, b_q_gain, b_w_o, rel_bias):
    bsz, seq, d = x.shape
    depth = norm_gain.shape[0]
    n_a = a_w_in.shape[0]
    t = bsz * seq

    ffn_weights = lambda layer, j: [(ffn_w_in, (layer, j)), (ffn_w_out, (layer, j))]
    stages = []
    for layer in range(depth):
        if layer == n_a:
            stages.append(("kv", layer, [(w_kv, ())]))
        if layer < n_a:
            stages.append(("ffn_hgrn2", layer,
                           ffn_weights(layer, 0) + [(a_w_in, (layer,)), (a_w_out, (layer,))]))
        else:
            stages.append(("ffn", (layer, 0), ffn_weights(layer, 0)))
            stages.append(("attention", layer, [(b_w_q, (layer - n_a,)), (b_w_o, (layer - n_a,))]))
        stages.append(("ffn", (layer, 2), ffn_weights(layer, 1)))
    carriers = ("kv", "ffn_hgrn2", "ffn", "attention")

    ready = [[arr[lead].astype(BF16) for arr, lead in stages[0][2]]]
    ks = vs = None
    for i, (kind, where, _) in enumerate(stages):
        own = ready.pop(0)
        ahead = []
        if kind in carriers:
            for later in stages[i + 1:]:
                ahead.append(later[2])
                if later[0] in carriers:
                    break
        casts = [w for ws in ahead for w in ws]
        if kind == "kv":
            (ks_vs, cast) = _project(x, kv_norm, own[0], k_gain, casts,
                                     n_kinds=2, n_normed=1, scale=1.0)
            ks, vs = ks_vs[:B_GROUPS], ks_vs[B_GROUPS:]
        elif kind == "ffn_hgrn2":
            x, cast = _ffn_hgrn2_layer(x.reshape(t, d), seq, norm_gain[where, 0], own[0], own[1],
                                       norm_gain[where, 1], own[2], a_lb_logits, where,
                                       a_out_gain[where], own[3], casts)
            x = x.reshape(bsz, seq, d)
        elif kind == "ffn":
            layer, slot = where
            x, cast = _ffn(x.reshape(t, d), norm_gain[layer, slot], own[0], own[1], casts)
            x = x.reshape(bsz, seq, d)
        else:
            j = where - n_a
            qs, cast = _project(x, norm_gain[where, 1], own[0], b_q_gain[j], casts,
                                n_kinds=1, n_normed=1, scale=B_HEAD_DIM ** -0.5 * LOG2E, tm=1024)
            parts = [_attn_group(qs[g], ks[g], vs[g], rel_bias, g) for g in range(B_GROUPS)]
            os, ms, ls = zip(*parts)
            x = _merge_out(x, os, ms, ls, own[1])
        cast = list(cast)
        for ws in ahead:
            ready.append([cast.pop(0) for _ in ws])
    return x
```
